```python
import math
import jax, jax.numpy as jnp
from jax import lax
import numpy as np

D_MODEL = 1024
BATCH = 32
SEQ = 2048
DEPTH = 2
DEC_BATCH = 8
DEC_SEQ = 32
PAST_LEN = 2048

CHUNK = 64
Q_BLOCK = 128
D_MIX = D_MODEL
A_HEADS = 8
A_HALF = 32
A_VDIM = 2 * A_HALF
B_HEADS = 4
B_DIM = 64
B_LEFT_CHUNKS = 8
B_REL_CLIP = 128
C_HEADS = 4
C_NOPE = 64
C_ROPE = 32
C_VDIM = 64
C_Q_LORA = 256
C_KV_LORA = 128
ROPE_THETA = 10000.0
T5_BUCKETS = 32
T5_MAX_DIST = 128
MEM_LEN = 256
M_HEADS = 4
M_DIM = 64
D_FF = 2816
CONV_W = 3
EPS = 1e-6
NEG = -1e30
N_MIX_PRE, N_MIX_POST, N_X_PRE, N_X_POST, N_F_PRE, N_F_POST, N_MEM = 0, 1, 2, 3, 4, 5, 6
N_NORMS = 7
IN_SIZES = (A_HEADS * 2 * A_HALF, A_HEADS * 2 * A_HALF, A_HEADS * A_VDIM,
            B_HEADS * B_DIM, B_HEADS * B_DIM, B_HEADS * B_DIM,
            C_Q_LORA, C_KV_LORA, C_ROPE)
IN_OFFSETS = tuple(int(v) for v in np.cumsum(IN_SIZES)[:-1])
D_IN = sum(IN_SIZES)

kernel_name = 'hybrid_chunk_stream_encoder_step'


def rms_norm(x, g):
    xf = x.astype(jnp.float32)
    y = xf * lax.rsqrt(jnp.mean(xf * xf, axis=-1, keepdims=True) + EPS)
    return (y * g.astype(jnp.float32)).astype(x.dtype)


def chunk_causal(qpos, kpos):
    return (kpos[None, :] // CHUNK) <= (qpos[:, None] // CHUNK)


def t5_bucket(rel):
    half = T5_BUCKETS // 2
    exact = half // 2
    ret = jnp.where(rel > 0, half, 0)
    n = jnp.abs(rel)
    nf = jnp.maximum(n, 1).astype(jnp.float32)
    large = exact + (jnp.log(nf / exact) / math.log(T5_MAX_DIST / exact) * (half - exact)).astype(jnp.int32)
    large = jnp.minimum(large, half - 1)
    return ret + jnp.where(n < exact, n, large)


def rope(x, pos):
    half = C_ROPE // 2
    inv = ROPE_THETA ** (-jnp.arange(half, dtype=jnp.float32) / half)
    ang = pos.astype(jnp.float32)[:, None] * inv[None, :]
    shape = (1, pos.shape[0]) + (1,) * (x.ndim - 3) + (half,)
    cos = jnp.cos(ang).reshape(shape)
    sin = jnp.sin(ang).reshape(shape)
    xf = x.astype(jnp.float32)
    x1, x2 = xf[..., :half], xf[..., half:]
    return jnp.concatenate([x1 * cos - x2 * sin, x2 * cos + x1 * sin], axis=-1).astype(x.dtype)


def split_in(h, w_in):
    B, T = h.shape[:2]
    aq, ak, av, bq, bk, bv, cq, ckv, cpe = jnp.split(h @ w_in, IN_OFFSETS, axis=-1)
    return dict(
        a_q=aq.reshape(B, T, A_HEADS, 2, A_HALF), a_k=ak.reshape(B, T, A_HEADS, 2, A_HALF),
        a_v=av.reshape(B, T, A_HEADS, A_VDIM),
        b_q=bq.reshape(B, T, B_HEADS, B_DIM), b_k=bk.reshape(B, T, B_HEADS, B_DIM),
        b_v=bv.reshape(B, T, B_HEADS, B_DIM),
        c_q=cq, c_kv=ckv, c_pe=cpe)


def blocked_queries(fn, qs, qpos):
    B, S = qs[0].shape[:2]
    n = S // Q_BLOCK
    blocks = tuple(jnp.moveaxis(q.reshape((B, n, Q_BLOCK) + q.shape[2:]), 1, 0) for q in qs)
    out = lax.map(lambda args: fn(args[0], *args[1:]), (qpos.reshape(n, Q_BLOCK),) + blocks)
    return jnp.moveaxis(out, 0, 1).reshape((B, S) + out.shape[3:])


def diff_lambda_value(lam, lam_init):
    lf = lam.astype(jnp.float32)
    return jnp.exp(jnp.sum(lf[0] * lf[1])) - jnp.exp(jnp.sum(lf[2] * lf[3])) + lam_init


def diff_attention(q, k, v, qpos, kpos, lam, t5_table):
    s = jnp.einsum('bqhcd,bkhcd->cbhqk', q, k).astype(jnp.float32) * (A_HALF ** -0.5)
    bias = jnp.transpose(t5_table[t5_bucket(kpos[None, :] - qpos[:, None])], (2, 0, 1)).astype(jnp.float32)
    s = jnp.where(chunk_causal(qpos, kpos), s + bias, NEG)
    p = jax.nn.softmax(s, axis=-1)
    attn = p[0] - lam * p[1]
    return jnp.einsum('bhqk,bkhd->bqhd', attn.astype(v.dtype), v)


def diff_finish(o, subln, lam_init):
    B, T = o.shape[:2]
    return (rms_norm(o, subln) * (1.0 - lam_init)).reshape(B, T, A_HEADS * A_VDIM)


def band_attention(q, k, v, qpos, kpos, rel_table):
    s = jnp.einsum('bcqhd,bckhd->bchqk', q, k).astype(jnp.float32) * (B_DIM ** -0.5)
    rel = jnp.clip(kpos[:, None, :] - qpos[:, :, None], -B_REL_CLIP, B_REL_CLIP) + B_REL_CLIP
    bias = jnp.transpose(rel_table[:, rel], (1, 0, 2, 3)).astype(jnp.float32)
    qc = qpos[:, :, None] // CHUNK
    kc = kpos[:, None, :] // CHUNK
    mask = (kpos[:, None, :] >= 0) & (kc <= qc) & (kc >= qc - B_LEFT_CHUNKS)
    s = jnp.where(mask[None, :, None], s + bias, NEG)
    p = jax.nn.softmax(s, axis=-1)
    return jnp.einsum('bchqk,bckhd->bcqhd', p.astype(v.dtype), v)


def band_prompt(q, k, v, rel_table):
    B, S = q.shape[:2]
    nc = S // CHUNK
    band = (B_LEFT_CHUNKS + 1) * CHUNK
    idx = jnp.arange(nc)[:, None] + jnp.arange(B_LEFT_CHUNKS + 1)[None, :]
    pad = ((0, 0), (B_LEFT_CHUNKS, 0), (0, 0), (0, 0), (0, 0))

    def chunks(t):
        return t.reshape(B, nc, CHUNK, B_HEADS, B_DIM)

    def gather_band(t):
        return jnp.pad(chunks(t), pad)[:, idx].reshape(B, nc, band, B_HEADS, B_DIM)

    qpos = jnp.arange(S, dtype=jnp.int32).reshape(nc, CHUNK)
    kpos = (jnp.arange(nc, dtype=jnp.int32)[:, None] - B_LEFT_CHUNKS) * CHUNK + jnp.arange(band, dtype=jnp.int32)[None, :]
    o = band_attention(chunks(q), gather_band(k), gather_band(v), qpos, kpos, rel_table)
    return o.reshape(B, S, B_HEADS * B_DIM)


def mla_queries(cq, pos, g_q, w_uq):
    B, T = cq.shape[:2]
    q = (rms_norm(cq, g_q) @ w_uq).reshape(B, T, C_HEADS, C_NOPE + C_ROPE)
    return q[..., :C_NOPE], rope(q[..., C_NOPE:], pos)


def mla_expand(latent, w_ukv):
    B, K = latent.shape[:2]
    kv = (latent @ w_ukv).reshape(B, K, C_HEADS, C_NOPE + C_VDIM)
    return kv[..., :C_NOPE], kv[..., C_NOPE:]


def mla_attention(q_nope, q_pe, k_nope, k_pe, v, qpos, kpos):
    s = (jnp.einsum('bqhd,bkhd->bhqk', q_nope, k_nope)
         + jnp.einsum('bqhr,bkr->bhqk', q_pe, k_pe)).astype(jnp.float32) * ((C_NOPE + C_ROPE) ** -0.5)
    s = jnp.where(chunk_causal(qpos, kpos), s, NEG)
    p = jax.nn.softmax(s, axis=-1)
    return jnp.einsum('bhqk,bkhd->bqhd', p.astype(v.dtype), v)


def mem_kv(mem, g, w_mk, w_mv):
    B = mem.shape[0]
    m = rms_norm(mem, g)
    return (m @ w_mk).reshape(B, MEM_LEN, M_HEADS, M_DIM), (m @ w_mv).reshape(B, MEM_LEN, M_HEADS, M_DIM)


def cross_attention(h, mk, mv, w_xq, w_xo):
    B, T = h.shape[:2]
    q = (h @ w_xq).reshape(B, T, M_HEADS, M_DIM)
    s = jnp.einsum('bqhd,bkhd->bhqk', q, mk.astype(q.dtype)).astype(jnp.float32) * (M_DIM ** -0.5)
    p = jax.nn.softmax(s, axis=-1)
    o = jnp.einsum('bhqk,bkhd->bqhd', p.astype(q.dtype), mv.astype(q.dtype)).reshape(B, T, M_HEADS * M_DIM)
    return o @ w_xo


def conv_ffn(h, past, w_gate, w_up, conv_w, conv_b, w_down):
    T = h.shape[1]
    g = h @ w_gate
    ext = jnp.concatenate([past.astype(g.dtype), g], axis=1)
    c = conv_b
    for j in range(CONV_W):
        c = c + ext[:, j:j + T] * conv_w[j]
    y = (jax.nn.silu(c) * (h @ w_up)) @ w_down
    return y, ext[:, ext.shape[1] - (CONV_W - 1):]


def tail_sublayers(x, mk, mv, conv_past, p):
    g = p['norms']
    x = x + rms_norm(cross_attention(rms_norm(x, g[N_X_PRE]), mk, mv, p['w_xq'], p['w_xo']), g[N_X_POST])
    f, conv_state = conv_ffn(rms_norm(x, g[N_F_PRE]), conv_past, p['w_gate'], p['w_up'],
                             p['conv_w'], p['conv_b'], p['w_down'])
    return x + rms_norm(f, g[N_F_POST]), conv_state


def layer_prompt(x, mem, p, t5_table, lam_init):
    B, S, _ = x.shape
    g = p['norms']
    pos = jnp.arange(S, dtype=jnp.int32)
    z = split_in(rms_norm(x, g[N_MIX_PRE]), p['w_in'])
    lam = diff_lambda_value(p['lam'], lam_init)
    a_k, a_v = z['a_k'], z['a_v']
    a = blocked_queries(lambda qp, q: diff_attention(q, a_k, a_v, qp, pos, lam, t5_table), (z['a_q'],), pos)
    a = diff_finish(a, p['subln'], lam_init)
    b = band_prompt(z['b_q'], z['b_k'], z['b_v'], p['rel'])
    qn, qe = mla_queries(z['c_q'], pos, p['g_q'], p['w_uq'])
    lat = rms_norm(z['c_kv'], p['g_kv'])
    kpe = rope(z['c_pe'], pos)
    kn, cv = mla_expand(lat, p['w_ukv'])
    c = blocked_queries(lambda qp, q1, q2: mla_attention(q1, q2, kn, kpe, cv, qp, pos), (qn, qe), pos)
    c = c.reshape(B, S, C_HEADS * C_VDIM)
    x = x + rms_norm(jnp.concatenate([a, b, c], axis=-1) @ p['w_out'], g[N_MIX_POST])
    mk, mv = mem_kv(mem, g[N_MEM], p['w_mk'], p['w_mv'])
    x, conv_state = tail_sublayers(x, mk, mv, jnp.zeros((B, CONV_W - 1, D_FF), x.dtype), p)
    nb = min(B_LEFT_CHUNKS * CHUNK, S)
    state = (a_k.reshape(B, S, A_HEADS, 2 * A_HALF), a_v, z['b_k'][:, S - nb:], z['b_v'][:, S - nb:],
             lat, kpe, mk, mv, conv_state)
    return x, state


def layer_sample(x, cache, p, t5_table, lam_init):
    ak_c, av_c, bk_c, bv_c, lat_c, kpe_c, mk, mv, conv_c = cache
    B, T, _ = x.shape
    P = ak_c.shape[1]
    nb = bk_c.shape[1]
    g = p['norms']
    pos = P + jnp.arange(T, dtype=jnp.int32)
    kpos = jnp.arange(P + T, dtype=jnp.int32)
    z = split_in(rms_norm(x, g[N_MIX_PRE]), p['w_in'])
    dt = z['a_q'].dtype
    lam = diff_lambda_value(p['lam'], lam_init)
    ak = jnp.concatenate([ak_c.reshape(B, P, A_HEADS, 2, A_HALF).astype(dt), z['a_k']], axis=1)
    av = jnp.concatenate([av_c.astype(dt), z['a_v']], axis=1)
    a = diff_finish(diff_attention(z['a_q'], ak, av, pos, kpos, lam, t5_table), p['subln'], lam_init)
    bk = jnp.concatenate([bk_c.astype(dt), z['b_k']], axis=1)
    bv = jnp.concatenate([bv_c.astype(dt), z['b_v']], axis=1)
    bkpos = (P - nb) + jnp.arange(nb + T, dtype=jnp.int32)
    b = band_attention(z['b_q'][:, None], bk[:, None], bv[:, None], pos[None], bkpos[None], p['rel'])[:, 0]
    b = b.reshape(B, T, B_HEADS * B_DIM)
    qn, qe = mla_queries(z['c_q'], pos, p['g_q'], p['w_uq'])
    lat_new = rms_norm(z['c_kv'], p['g_kv'])
    kpe_new = rope(z['c_pe'], pos)
    lat = jnp.concatenate([lat_c.astype(dt), lat_new], axis=1)
    kpe = jnp.concatenate([kpe_c.astype(dt), kpe_new], axis=1)
    kn, cv = mla_expand(lat, p['w_ukv'])
    c = mla_attention(qn, qe, kn, kpe, cv, pos, kpos).reshape(B, T, C_HEADS * C_VDIM)
    x = x + rms_norm(jnp.concatenate([a, b, c], axis=-1) @ p['w_out'], g[N_MIX_POST])
    x, conv_state = tail_sublayers(x, mk, mv, conv_c, p)
    state = (z['a_k'].reshape(B, T, A_HEADS, 2 * A_HALF), z['a_v'], z['b_k'], z['b_v'],
             lat_new, kpe_new, conv_state)
    return x, state


def setup_inputs(seed: int = 0) -> dict:
    key = jax.random.key(seed)
    ks = iter(jax.random.split(key, 40))

    def nrm(shape, scale=1.0):
        return jax.random.normal(next(ks), shape, jnp.float32) * scale

    def gain(shape):
        return 1.0 + nrm(shape, 0.05)

    nb = min(B_LEFT_CHUNKS * CHUNK, PAST_LEN)
    return {
        'x_prompt': nrm((BATCH, SEQ, D_MODEL)),
        'x_sample': nrm((DEC_BATCH, DEC_SEQ, D_MODEL)),
        'cache_a_k': nrm((DEPTH, DEC_BATCH, PAST_LEN, A_HEADS, 2 * A_HALF)),
        'cache_a_v': nrm((DEPTH, DEC_BATCH, PAST_LEN, A_HEADS, A_VDIM)),
        'cache_b_k': nrm((DEPTH, DEC_BATCH, nb, B_HEADS, B_DIM)),
        'cache_b_v': nrm((DEPTH, DEC_BATCH, nb, B_HEADS, B_DIM)),
        'cache_c_latent': nrm((DEPTH, DEC_BATCH, PAST_LEN, C_KV_LORA)),
        'cache_c_rope_k': nrm((DEPTH, DEC_BATCH, PAST_LEN, C_ROPE)),
        'cache_mem_k': nrm((DEPTH, DEC_BATCH, MEM_LEN, M_HEADS, M_DIM)),
        'cache_mem_v': nrm((DEPTH, DEC_BATCH, MEM_LEN, M_HEADS, M_DIM)),
        'state_ffn_conv': nrm((DEPTH, DEC_BATCH, CONV_W - 1, D_FF)),
        'mem_prompt': nrm((BATCH, MEM_LEN, D_MODEL)),
        'w_in': nrm((DEPTH, D_MODEL, D_IN), D_MODEL ** -0.5),
        'w_out': nrm((DEPTH, D_MIX, D_MODEL), D_MIX ** -0.5),
        'norms': gain((DEPTH, N_NORMS, D_MODEL)),
        'diff_lambda': nrm((DEPTH, 4, A_HALF), 0.1),
        'diff_subln': gain((DEPTH, A_VDIM)),
        't5_bias': nrm((T5_BUCKETS, A_HEADS), 0.5),
        'band_rel_bias': nrm((DEPTH, B_HEADS, 2 * B_REL_CLIP + 1), 0.5),
        'mla_q_norm': gain((DEPTH, C_Q_LORA)),
        'mla_w_uq': nrm((DEPTH, C_Q_LORA, C_HEADS * (C_NOPE + C_ROPE)), C_Q_LORA ** -0.5),
        'mla_kv_norm': gain((DEPTH, C_KV_LORA)),
        'mla_w_ukv': nrm((DEPTH, C_KV_LORA, C_HEADS * (C_NOPE + C_VDIM)), C_KV_LORA ** -0.5),
        'w_xq': nrm((DEPTH, D_MODEL, M_HEADS * M_DIM), D_MODEL ** -0.5),
        'w_mk': nrm((DEPTH, D_MODEL, M_HEADS * M_DIM), D_MODEL ** -0.5),
        'w_mv': nrm((DEPTH, D_MODEL, M_HEADS * M_DIM), D_MODEL ** -0.5),
        'w_xo': nrm((DEPTH, M_HEADS * M_DIM, D_MODEL), (M_HEADS * M_DIM) ** -0.5),
        'w_gate': nrm((DEPTH, D_MODEL, D_FF), D_MODEL ** -0.5),
        'w_up': nrm((DEPTH, D_MODEL, D_FF), D_MODEL ** -0.5),
        'conv_w': nrm((DEPTH, CONV_W, D_FF), CONV_W ** -0.5),
        'conv_b': nrm((DEPTH, D_FF), 0.01),
        'w_down': nrm((DEPTH, D_FF, D_MODEL), D_FF ** -0.5),
    }


def reference(x_prompt, x_sample, cache_a_k, cache_a_v, cache_b_k, cache_b_v, cache_c_latent,
              cache_c_rope_k, cache_mem_k, cache_mem_v, state_ffn_conv, mem_prompt,
              w_in, w_out, norms, diff_lambda, diff_subln, t5_bias, band_rel_bias,
              mla_q_norm, mla_w_uq, mla_kv_norm, mla_w_ukv, w_xq, w_mk, w_mv, w_xo,
              w_gate, w_up, conv_w, conv_b, w_down):
    xp, xs = x_prompt, x_sample
    sp_all, ss_all = [], []
    for l in range(DEPTH):
        p = dict(w_in=w_in[l], w_out=w_out[l], norms=norms[l], lam=diff_lambda[l], subln=diff_subln[l],
                 rel=band_rel_bias[l], g_q=mla_q_norm[l], w_uq=mla_w_uq[l], g_kv=mla_kv_norm[l],
                 w_ukv=mla_w_ukv[l], w_xq=w_xq[l], w_mk=w_mk[l], w_mv=w_mv[l], w_xo=w_xo[l],
                 w_gate=w_gate[l], w_up=w_up[l], conv_w=conv_w[l], conv_b=conv_b[l], w_down=w_down[l])
        lam_init = 0.8 - 0.6 * math.exp(-0.3 * l)
        xp, sp = layer_prompt(xp, mem_prompt, p, t5_bias, lam_init)
        cache_l = (cache_a_k[l], cache_a_v[l], cache_b_k[l], cache_b_v[l], cache_c_latent[l],
                   cache_c_rope_k[l], cache_mem_k[l], cache_mem_v[l], state_ffn_conv[l])
        xs, ss = layer_sample(xs, cache_l, p, t5_bias, lam_init)
        sp_all.append(sp)
        ss_all.append(ss)

    def stk(states, i):
        return jnp.stack([s[i] for s in states])

    return (xp, xs,
            stk(sp_all, 0), stk(sp_all, 1), stk(sp_all, 2), stk(sp_all, 3), stk(sp_all, 4),
            stk(sp_all, 5), stk(sp_all, 6), stk(sp_all, 7), stk(sp_all, 8),
            stk(ss_all, 0), stk(ss_all, 1), stk(ss_all, 2), stk(ss_all, 3), stk(ss_all, 4),
            stk(ss_all, 5), stk(ss_all, 6))
```

```python
import functools
import math

import jax
import jax.numpy as jnp
import numpy as np
from jax import lax
from jax.experimental import pallas as pl
from jax.experimental.pallas import tpu as pltpu

D_MODEL = 1024
DEPTH = 2
CHUNK = 64
A_HEADS, A_HALF, A_VDIM = 8, 32, 64
B_HEADS, B_DIM, B_LEFT_CHUNKS, B_REL_CLIP = 4, 64, 8, 128
C_HEADS, C_NOPE, C_ROPE, C_VDIM, C_Q_LORA, C_KV_LORA = 4, 64, 32, 64, 256, 128
ROPE_THETA = 10000.0
T5_BUCKETS = 32
MEM_LEN, M_HEADS, M_DIM = 256, 4, 64
D_FF = 2816
CONV_W = 3
EPS = 1e-6
NEG = -1e30
N_MIX_PRE, N_MIX_POST, N_X_PRE, N_X_POST, N_F_PRE, N_F_POST, N_MEM = 0, 1, 2, 3, 4, 5, 6

A_SCALE = A_HALF ** -0.5
B_SCALE = B_DIM ** -0.5
C_SCALE = (C_NOPE + C_ROPE) ** -0.5
M_SCALE = M_DIM ** -0.5

LANES = 128
C_HEAD_PAD = 128
ATT_TILE = 256
FF_CHUNK = 256
VMEM_LIMIT = 56 * 1024 * 1024

BF16 = jnp.bfloat16
F32 = jnp.float32


def _params(n_grid):
    return pltpu.CompilerParams(dimension_semantics=("arbitrary",) * n_grid, vmem_limit_bytes=VMEM_LIMIT)


def _dot(a, b):
    return jnp.dot(a, b, preferred_element_type=F32)


def _dot_nt(a, b):
    return lax.dot_general(a, b, (((1,), (1,)), ((), ())), preferred_element_type=F32)


def _rms(x, g):
    ms = jnp.mean(x * x, axis=-1, keepdims=True)
    return x * lax.rsqrt(ms + EPS) * g


def _full(shape):
    n = len(shape)
    return pl.BlockSpec(shape, lambda *_: (0,) * n)


def _row_tile(rows, cap):
    t = min(rows, cap)
    assert rows % t == 0
    return t


def _mla_expand(lat, kpe, wkn_ref, ek_ref, wcv_ref):
    lat16 = lat.astype(BF16)
    kcat = _dot(lat16, wkn_ref[...]) + _dot(kpe.astype(BF16), ek_ref[...])
    cv = _dot(lat16, wcv_ref[...])
    return kcat.astype(BF16), cv.astype(BF16)


def _inproj_kernel(x_ref, g_ref, cos_ref, sin_ref, wa_ref, wb_ref, wc_ref, gq_ref, wuq_ref, eq_ref,
                   gkv_ref, wkn_ref, ek_ref, wcv_ref,
                   aq_o, ak32_o, ak16_o, av32_o, av16_o, bq_o, bk32_o, bk16_o, bv32_o, bv16_o,
                   cq_o, lat_o, kpe_o, kcat_o, cv_o):
    h = _rms(x_ref[0], g_ref[...]).astype(BF16)
    na = A_HEADS * 2 * A_HALF
    aq_o[0] = (_dot(h, wa_ref[:, 0:na]) * A_SCALE).astype(BF16)
    ak = _dot(h, wa_ref[:, na:2 * na])
    ak32_o[0] = ak
    ak16_o[0] = ak.astype(BF16)
    av = _dot(h, wa_ref[:, 2 * na:3 * na])
    av32_o[0] = av
    av16_o[0] = av.astype(BF16)
    nb = B_HEADS * B_DIM
    bq_o[0] = (_dot(h, wb_ref[:, 0:nb]) * B_SCALE).astype(BF16)
    bk = _dot(h, wb_ref[:, nb:2 * nb])
    bk32_o[0] = bk
    bk16_o[0] = bk.astype(BF16)
    bv = _dot(h, wb_ref[:, 2 * nb:3 * nb])
    bv32_o[0] = bv
    bv16_o[0] = bv.astype(BF16)
    zc = _dot(h, wc_ref[...])
    cos4 = cos_ref[...]
    sin4 = sin_ref[...]
    hq = _rms(zc[:, 0:C_Q_LORA], gq_ref[...]).astype(BF16)
    nq = C_HEADS * C_HEAD_PAD
    zq = _dot(hq, wuq_ref[...])
    qpe = (zq[:, nq:nq + LANES] * cos4 + zq[:, nq + LANES:nq + 2 * LANES] * sin4) * C_SCALE
    cq_o[0] = (zq[:, 0:nq] * C_SCALE + _dot(qpe.astype(BF16), eq_ref[...])).astype(BF16)
    lat = _rms(zc[:, C_Q_LORA:C_Q_LORA + C_KV_LORA], gkv_ref[...])
    lat_o[0] = lat
    o = C_Q_LORA + C_KV_LORA
    kpe = zc[:, o:o + C_ROPE] * cos4[:, 0:C_ROPE] + zc[:, o + C_ROPE:o + 2 * C_ROPE] * sin4[:, 0:C_ROPE]
    kpe_o[0] = kpe
    kcat, cv = _mla_expand(lat, kpe, wkn_ref, ek_ref, wcv_ref)
    kcat_o[0] = kcat
    cv_o[0] = cv


def _inproj(x, lw, cos4, sin4):
    B, S, _ = x.shape
    tm = _row_tile(S, 512)
    grid = (B, S // tm)
    row = lambda w: pl.BlockSpec((1, tm, w), lambda b, i: (b, i, 0))
    pos = pl.BlockSpec((tm, LANES), lambda b, i: (i, 0))
    weights = [lw['g_mix_pre'], cos4, sin4, lw['wa'], lw['wb'], lw['wc'], lw['g_q'], lw['wuq'], lw['eq'],
               lw['g_kv'], lw['wkn'], lw['ek'], lw['wcv']]
    in_specs = [row(D_MODEL)] + [pos if w is cos4 or w is sin4 else _full(w.shape) for w in weights]
    widths = [(512, BF16), (512, F32), (512, BF16), (512, F32), (512, BF16),
              (256, BF16), (256, F32), (256, BF16), (256, F32), (256, BF16),
              (C_HEADS * C_HEAD_PAD, BF16), (C_KV_LORA, F32), (C_ROPE, F32),
              (C_HEADS * C_HEAD_PAD, BF16), (C_HEADS * C_VDIM, BF16)]
    return pl.pallas_call(
        _inproj_kernel, grid=grid, in_specs=in_specs,
        out_specs=[row(w) for w, _ in widths],
        out_shape=[jax.ShapeDtypeStruct((B, S, w), d) for w, d in widths],
        compiler_params=_params(2), name="inproj")(x, *weights)


def _expand_kernel(lat_ref, kpe_ref, wkn_ref, ek_ref, wcv_ref, kcat_o, cv_o):
    kcat, cv = _mla_expand(lat_ref[0], kpe_ref[0], wkn_ref, ek_ref, wcv_ref)
    kcat_o[0] = kcat
    cv_o[0] = cv


def _expand_cache(lat, kpe, lw):
    B, P, _ = lat.shape
    tm = _row_tile(P, 512)
    row = lambda w: pl.BlockSpec((1, tm, w), lambda b, i: (b, i, 0))
    ws = [lw['wkn'], lw['ek'], lw['wcv']]
    return pl.pallas_call(
        _expand_kernel, grid=(B, P // tm),
        in_specs=[row(C_KV_LORA), row(C_ROPE)] + [_full(w.shape) for w in ws],
        out_specs=[row(C_HEADS * C_HEAD_PAD), row(C_HEADS * C_VDIM)],
        out_shape=[jax.ShapeDtypeStruct((B, P, C_HEADS * C_HEAD_PAD), BF16),
                   jax.ShapeDtypeStruct((B, P, C_HEADS * C_VDIM), BF16)],
        compiler_params=_params(2), name="mla_expand")(lat, kpe, *ws)


def _lane_iota():
    return lax.broadcasted_iota(jnp.int32, (1, LANES), 1)


def _masked_streams(q, width):
    lane = _lane_iota()
    return [jnp.where(lane // width == s, q, jnp.zeros_like(q)) for s in range(LANES // width)]


def _online_update(sc, vt, m_ref, l_ref, acc_ref, s):
    reps = sc.shape[1] // LANES
    m_prev = m_ref[s]
    m_next = jnp.maximum(m_prev, jnp.max(sc, axis=1, keepdims=True))
    alpha = jnp.exp(m_prev - m_next)
    p = jnp.exp(sc - jnp.concatenate([m_next] * reps, axis=1))
    l_ref[s] = alpha * l_ref[s] + jnp.sum(p, axis=1, keepdims=True)
    acc_ref[s] = alpha * acc_ref[s] + _dot(p.astype(BF16), vt)
    m_ref[s] = m_next


def _apply_bias(sc, bt):
    return jnp.where(bt > 0.5 * NEG, sc + bt, NEG)


def _softmax_pv(sc, vt):
    m = jnp.max(sc, axis=1, keepdims=True)
    p = jnp.exp(sc - m)
    l = jnp.sum(p, axis=1, keepdims=True)
    return _dot(p.astype(BF16), vt) / l


def _init_state(m_ref, l_ref, acc_ref):
    m_ref[...] = jnp.full(m_ref.shape, NEG, F32)
    l_ref[...] = jnp.zeros(l_ref.shape, F32)
    acc_ref[...] = jnp.zeros(acc_ref.shape, F32)


def _diff_lambda(lam_ref, lam_init):
    lf = lam_ref[...]
    return (jnp.exp(jnp.sum(lf[0:1] * lf[1:2], axis=1, keepdims=True))
            - jnp.exp(jnp.sum(lf[2:3] * lf[3:4], axis=1, keepdims=True)) + lam_init)


def _finish_diff(outs, lam, subln_ref, lam_init):
    lane = _lane_iota()
    first = lane < A_VDIM
    o = jnp.where(first, outs[0] - lam * outs[1], outs[2] - lam * outs[3])
    sq = o * o
    ss0 = jnp.sum(jnp.where(first, sq, 0.0), axis=1, keepdims=True)
    ss1 = jnp.sum(jnp.where(first, 0.0, sq), axis=1, keepdims=True)
    ms = jnp.where(first, ss0, ss1) * (1.0 / A_VDIM)
    return (o * lax.rsqrt(ms + EPS) * subln_ref[...]) * (1.0 - lam_init)


def _attn_a_kernel(cfar_ref, lam_ref, subln_ref, q_ref, k_ref, v_ref, bias_ref, o_ref,
                   m_ref, l_ref, acc_ref, *, lam_init, t):
    pair = pl.program_id(1)
    i = pl.program_id(2)
    qs = _masked_streams(q_ref[0], A_HALF)
    _init_state(m_ref, l_ref, acc_ref)

    def tile(j, kind):
        start = pl.multiple_of(j * t, t)
        kt = k_ref[0, pl.ds(start, t), :]
        vt = v_ref[0, pl.ds(start, t), :]
        for s in range(4):
            sc = _dot_nt(qs[s], kt)
            if kind == 'far':
                sc = sc + cfar_ref[2 * pair + s // 2]
            elif kind == 'near':
                sc = sc + bias_ref[s // 2, 1]
            else:
                sc = _apply_bias(sc, bias_ref[s // 2, 0])
            _online_update(sc, vt, m_ref, l_ref, acc_ref, s)

    tile(i, 'diag')

    @pl.when(i >= 1)
    def _():
        tile(i - 1, 'near')

    def far_body(j, carry):
        tile(j, 'far')
        return carry

    lax.fori_loop(0, jnp.maximum(i - 1, 0), far_body, 0)
    outs = [acc_ref[s] / l_ref[s] for s in range(4)]
    o_ref[0] = _finish_diff(outs, _diff_lambda(lam_ref, lam_init), subln_ref, lam_init).astype(BF16)


def _attn_a(aq, ak, av, bias, cfar, lam, subln2, lam_init):
    B, S, _ = aq.shape
    t = ATT_TILE
    assert S % t == 0 and t + 1 >= 128
    kernel = functools.partial(_attn_a_kernel, lam_init=lam_init, t=t)
    return pl.pallas_call(
        kernel, grid=(B, A_HEADS // 2, S // t),
        in_specs=[pl.BlockSpec(memory_space=pltpu.SMEM), _full(lam.shape), _full(subln2.shape),
                  pl.BlockSpec((1, t, LANES), lambda b, p, i: (b, i, p)),
                  pl.BlockSpec((1, S, LANES), lambda b, p, i: (b, 0, p)),
                  pl.BlockSpec((1, S, LANES), lambda b, p, i: (b, 0, p)),
                  pl.BlockSpec((2, 2, t, t), lambda b, p, i: (p, 0, 0, 0))],
        out_specs=pl.BlockSpec((1, t, LANES), lambda b, p, i: (b, i, p)),
        out_shape=jax.ShapeDtypeStruct((B, S, A_HEADS * A_VDIM), BF16),
        scratch_shapes=[pltpu.VMEM((4, t, LANES), F32)] * 3,
        compiler_params=_params(3), name="attn_diff")(cfar, lam, subln2, aq, ak, av, bias)


def _attn_b_kernel(q_ref, k_ref, v_ref, bias_ref, o_ref, m_ref, l_ref, acc_ref, *, t, n_back):
    i = pl.program_id(2)
    qs = _masked_streams(q_ref[0], B_DIM)
    _init_state(m_ref, l_ref, acc_ref)

    def tile(back):
        start = pl.multiple_of((i - back) * t, t)
        kt = k_ref[0, pl.ds(start, t), :]
        vt = v_ref[0, pl.ds(start, t), :]
        for s in range(2):
            sc = _apply_bias(_dot_nt(qs[s], kt), bias_ref[s, back])
            _online_update(sc, vt, m_ref, l_ref, acc_ref, s)

    tile(0)
    for back in range(1, n_back + 1):
        pl.when(i >= back)(functools.partial(tile, back))
    o_ref[0] = jnp.where(_lane_iota() < B_DIM, acc_ref[0] / l_ref[0], acc_ref[1] / l_ref[1]).astype(BF16)


def _attn_b(bq, bk, bv, bias):
    B, S, _ = bq.shape
    t = ATT_TILE
    n_back = bias.shape[1] - 1
    assert S % t == 0
    kernel = functools.partial(_attn_b_kernel, t=t, n_back=n_back)
    return pl.pallas_call(
        kernel, grid=(B, B_HEADS // 2, S // t),
        in_specs=[pl.BlockSpec((1, t, LANES), lambda b, p, i: (b, i, p)),
                  pl.BlockSpec((1, S, LANES), lambda b, p, i: (b, 0, p)),
                  pl.BlockSpec((1, S, LANES), lambda b, p, i: (b, 0, p)),
                  pl.BlockSpec((2, n_back + 1, t, t), lambda b, p, i: (p, 0, 0, 0))],
        out_specs=pl.BlockSpec((1, t, LANES), lambda b, p, i: (b, i, p)),
        out_shape=jax.ShapeDtypeStruct((B, S, B_HEADS * B_DIM), BF16),
        scratch_shapes=[pltpu.VMEM((2, t, LANES), F32)] * 3,
        compiler_params=_params(3), name="attn_band")(bq, bk, bv, bias)


def _attn_c_kernel(q_ref, k_ref, v_ref, mask_ref, o_ref, m_ref, l_ref, acc_ref, *, t):
    i = pl.program_id(2)
    q = q_ref[0]
    qs = [q[:, 0:C_HEAD_PAD], q[:, C_HEAD_PAD:2 * C_HEAD_PAD]]
    _init_state(m_ref, l_ref, acc_ref)

    def tile(j, diag):
        start = pl.multiple_of(j * t, t)
        kt = k_ref[0, pl.ds(start, t), :]
        vt = v_ref[0, pl.ds(start, t), :]
        for s in range(2):
            sc = _dot_nt(qs[s], kt[:, s * C_HEAD_PAD:(s + 1) * C_HEAD_PAD])
            if diag:
                sc = _apply_bias(sc, mask_ref[...])
            _online_update(sc, vt, m_ref, l_ref, acc_ref, s)

    tile(i, True)

    def far_body(j, carry):
        tile(j, False)
        return carry

    lax.fori_loop(0, i, far_body, 0)
    o_ref[0] = jnp.where(_lane_iota() < C_VDIM, acc_ref[0] / l_ref[0], acc_ref[1] / l_ref[1]).astype(BF16)


def _attn_c(cq, kcat, cv, mask):
    B, S, _ = cq.shape
    t = ATT_TILE
    assert S % t == 0
    kernel = functools.partial(_attn_c_kernel, t=t)
    return pl.pallas_call(
        kernel, grid=(B, C_HEADS // 2, S // t),
        in_specs=[pl.BlockSpec((1, t, 2 * C_HEAD_PAD), lambda b, p, i: (b, i, p)),
                  pl.BlockSpec((1, S, 2 * C_HEAD_PAD), lambda b, p, i: (b, 0, p)),
                  pl.BlockSpec((1, S, LANES), lambda b, p, i: (b, 0, p)),
                  _full(mask.shape)],
        out_specs=pl.BlockSpec((1, t, LANES), lambda b, p, i: (b, i, p)),
        out_shape=jax.ShapeDtypeStruct((B, S, C_HEADS * C_VDIM), BF16),
        scratch_shapes=[pltpu.VMEM((2, t, LANES), F32)] * 3,
        compiler_params=_params(3), name="attn_mla")(cq, kcat, cv, mask)


def _dense_attn_kernel(*refs, mode, lam_init):
    if mode == 'a':
        lam_ref, subln_ref, q_ref, k_ref, v_ref, bias_ref, o_ref = refs
    else:
        q_ref, k_ref, v_ref, bias_ref, o_ref = refs
    q = q_ref[0]
    k = k_ref[0]
    vt = v_ref[0]
    if mode == 'a':
        streams = [(qm, k, s // 2) for s, qm in enumerate(_masked_streams(q, A_HALF))]
    elif mode == 'b':
        streams = [(qm, k, s) for s, qm in enumerate(_masked_streams(q, B_DIM))]
    else:
        streams = [(q[:, s * C_HEAD_PAD:(s + 1) * C_HEAD_PAD], k[:, s * C_HEAD_PAD:(s + 1) * C_HEAD_PAD], s)
                   for s in range(2)]
    outs = [_softmax_pv(_apply_bias(_dot_nt(qm, km), bias_ref[hh]), vt) for qm, km, hh in streams]
    if mode == 'a':
        o = _finish_diff(outs, _diff_lambda(lam_ref, lam_init), subln_ref, lam_init)
    else:
        o = jnp.where(_lane_iota() < 64, outs[0], outs[1])
    o_ref[0] = o.astype(BF16)


def _dense_attn(mode, q, k, v, bias, n_heads, extra=(), lam_init=0.0):
    B, T, _ = q.shape
    Kp = k.shape[1]
    qw = 2 * C_HEAD_PAD if mode == 'c' else LANES
    kernel = functools.partial(_dense_attn_kernel, mode=mode, lam_init=lam_init)
    return pl.pallas_call(
        kernel, grid=(B, n_heads // 2),
        in_specs=[_full(e.shape) for e in extra] + [
            pl.BlockSpec((1, T, qw), lambda b, p: (b, 0, p)),
            pl.BlockSpec((1, Kp, qw), lambda b, p: (b, 0, p)),
            pl.BlockSpec((1, Kp, LANES), lambda b, p: (b, 0, p)),
            pl.BlockSpec((2, T, Kp), lambda b, p: (p, 0, 0))],
        out_specs=pl.BlockSpec((1, T, LANES), lambda b, p: (b, 0, p)),
        out_shape=jax.ShapeDtypeStruct((B, T, n_heads * 64), BF16),
        compiler_params=_params(2), name="attn_dense_" + mode)(*extra, q, k, v, bias)


def _outproj_kernel(x_ref, a_ref, b_ref, c_ref, w_ref, g_ref, o_ref):
    na = A_HEADS * A_VDIM
    nb = na + B_HEADS * B_DIM
    y = (_dot(a_ref[0], w_ref[0:na, :]) + _dot(b_ref[0], w_ref[na:nb, :]) + _dot(c_ref[0], w_ref[nb:D_MODEL, :]))
    o_ref[0] = x_ref[0] + _rms(y, g_ref[...])


def _outproj(x, a, b, c, w, g):
    B, S, _ = x.shape
    tm = _row_tile(S, 512)
    row = lambda wd: pl.BlockSpec((1, tm, wd), lambda bb, i: (bb, i, 0))
    return pl.pallas_call(
        _outproj_kernel, grid=(B, S // tm),
        in_specs=[row(D_MODEL), row(a.shape[2]), row(b.shape[2]), row(c.shape[2]), _full(w.shape), _full(g.shape)],
        out_specs=row(D_MODEL), out_shape=jax.ShapeDtypeStruct(x.shape, F32),
        compiler_params=_params(2), name="outproj")(x, a, b, c, w, g)


def _memkv_kernel(mem_ref, g_ref, wk_ref, wv_ref, k32_o, k16_o, v32_o, v16_o):
    m = _rms(mem_ref[0], g_ref[...]).astype(BF16)
    mk = _dot(m, wk_ref[...])
    mv = _dot(m, wv_ref[...])
    k32_o[0] = mk
    k16_o[0] = mk.astype(BF16)
    v32_o[0] = mv
    v16_o[0] = mv.astype(BF16)


def _memkv(mem, g, wk, wv):
    B = mem.shape[0]
    n = M_HEADS * M_DIM
    blk = pl.BlockSpec((1, MEM_LEN, n), lambda b: (b, 0, 0))
    return pl.pallas_call(
        _memkv_kernel, grid=(B,),
        in_specs=[pl.BlockSpec((1, MEM_LEN, D_MODEL), lambda b: (b, 0, 0)), _full(g.shape), _full(wk.shape),
                  _full(wv.shape)],
        out_specs=[blk] * 4,
        out_shape=[jax.ShapeDtypeStruct((B, MEM_LEN, n), d) for d in (F32, BF16, F32, BF16)],
        compiler_params=_params(1), name="mem_kv")(mem, g, wk, wv)


def _xattn_kernel(x_ref, mk_ref, mv_ref, gpre_ref, wq_ref, wo_ref, gpost_ref, o_ref):
    x = x_ref[0]
    h = _rms(x, gpre_ref[...]).astype(BF16)
    q = (_dot(h, wq_ref[...]) * M_SCALE).astype(BF16)
    lane = _lane_iota()
    pairs = []
    for p in range(M_HEADS // 2):
        sl = slice(p * LANES, (p + 1) * LANES)
        kp = mk_ref[0, :, sl]
        vp = mv_ref[0, :, sl]
        outs = [_softmax_pv(_dot_nt(qm, kp), vp) for qm in _masked_streams(q[:, sl], M_DIM)]
        pairs.append(jnp.where(lane < M_DIM, outs[0], outs[1]))
    o = jnp.concatenate(pairs, axis=1).astype(BF16)
    o_ref[0] = x + _rms(_dot(o, wo_ref[...]), gpost_ref[...])


def _xattn(x, mk, mv, gpre, wq, wo, gpost):
    B, S, _ = x.shape
    tm = _row_tile(S, 512)
    n = M_HEADS * M_DIM
    row = pl.BlockSpec((1, tm, D_MODEL), lambda b, i: (b, i, 0))
    mem = pl.BlockSpec((1, MEM_LEN, n), lambda b, i: (b, 0, 0))
    return pl.pallas_call(
        _xattn_kernel, grid=(B, S // tm),
        in_specs=[row, mem, mem, _full(gpre.shape), _full(wq.shape), _full(wo.shape), _full(gpost.shape)],
        out_specs=row, out_shape=jax.ShapeDtypeStruct(x.shape, F32),
        compiler_params=_params(2), name="xattn")(x, mk, mv, gpre, wq, wo, gpost)


def _ffn_kernel(x_ref, halo_ref, past_ref, gpre_ref, wg_ref, wu_ref, cw_ref, cb_ref, wd_ref, gpost_ref,
                o_ref, st_ref, *, tm):
    i = pl.program_id(1)
    x = x_ref[0]
    gpre = gpre_ref[...]
    h = _rms(x, gpre).astype(BF16)
    hh = _rms(halo_ref[0], gpre).astype(BF16)
    row = lax.broadcasted_iota(jnp.int32, (tm, 1), 0)
    first = i == 0
    acc = jnp.zeros((tm, D_MODEL), F32)
    for c in range(D_FF // FF_CHUNK):
        sl = slice(c * FF_CHUNK, (c + 1) * FF_CHUNK)
        wg = wg_ref[:, sl]
        g = _dot(h, wg)
        gh = _dot(hh, wg)
        past = past_ref[0, :, sl]
        p1 = jnp.where(first, past[1:2], gh[7:8])
        p2 = jnp.where(first, past[0:1], gh[6:7])
        g1 = jnp.where(row == 0, p1, pltpu.roll(g, 1, axis=0))
        g2 = jnp.where(row == 0, p2, jnp.where(row == 1, p1, pltpu.roll(g, 2, axis=0)))
        cw = cw_ref[:, sl]
        cc = cb_ref[:, sl] + g2 * cw[0:1] + g1 * cw[1:2] + g * cw[2:3]
        u = _dot(h, wu_ref[:, sl])
        a = (cc * jax.nn.sigmoid(cc) * u).astype(BF16)
        acc = acc + _dot(a, wd_ref[sl, :])
        st_ref[0, :, sl] = g[tm - 2:tm]
    o_ref[0] = x + _rms(acc, gpost_ref[...])


def _ffn(x, past, gpre, wg, wu, cw, cb, wd, gpost):
    B, S, _ = x.shape
    tm = _row_tile(S, 512)
    assert tm % 8 == 0
    hb = tm // 8
    row = pl.BlockSpec((1, tm, D_MODEL), lambda b, i: (b, i, 0))
    halo = pl.BlockSpec((1, 8, D_MODEL), lambda b, i: (b, jnp.maximum(i * hb - 1, 0), 0))
    st = pl.BlockSpec((1, CONV_W - 1, D_FF), lambda b, i: (b, 0, 0))
    ws = [gpre, wg, wu, cw, cb, wd, gpost]
    return pl.pallas_call(
        functools.partial(_ffn_kernel, tm=tm), grid=(B, S // tm),
        in_specs=[row, halo, st] + [_full(w.shape) for w in ws],
        out_specs=[row, st],
        out_shape=[jax.ShapeDtypeStruct(x.shape, F32), jax.ShapeDtypeStruct((B, CONV_W - 1, D_FF), F32)],
        compiler_params=_params(2), name="ffn")(x, x, past, *ws)


def _rot_cols(w):
    half = C_ROPE // 2
    return jnp.concatenate([-w[..., half:], w[..., :half]], axis=-1)


def _layer_weights(l, w_in, w_out, norms, mla_q_norm, mla_w_uq, mla_kv_norm, mla_w_ukv, w_xq, w_mk, w_mv, w_xo,
                   w_gate, w_up, conv_w, conv_b, w_down):
    wi = w_in[l]
    na, nb = 3 * A_HEADS * 2 * A_HALF, 3 * B_HEADS * B_DIM
    wcq = wi[:, na + nb:na + nb + C_Q_LORA]
    wckv = wi[:, na + nb + C_Q_LORA:na + nb + C_Q_LORA + C_KV_LORA]
    wcpe = wi[:, na + nb + C_Q_LORA + C_KV_LORA:]
    wc = jnp.concatenate([wcq, wckv, wcpe, _rot_cols(wcpe),
                          jnp.zeros((D_MODEL, 512 - C_Q_LORA - C_KV_LORA - 2 * C_ROPE), F32)], axis=1)
    uq = mla_w_uq[l].reshape(C_Q_LORA, C_HEADS, C_NOPE + C_ROPE)
    uq_n = jnp.pad(uq[..., :C_NOPE], ((0, 0), (0, 0), (0, C_HEAD_PAD - C_NOPE))).reshape(C_Q_LORA, -1)
    uq_p = uq[..., C_NOPE:]
    wuq = jnp.concatenate([uq_n, uq_p.reshape(C_Q_LORA, -1), _rot_cols(uq_p).reshape(C_Q_LORA, -1)], axis=1)
    eye = jnp.eye(C_ROPE, dtype=F32)
    place = jnp.pad(eye, ((0, 0), (C_NOPE, C_HEAD_PAD - C_NOPE - C_ROPE)))
    eq = jnp.kron(jnp.eye(C_HEADS, dtype=F32), place)
    ek = jnp.tile(place, (1, C_HEADS))
    ukv = mla_w_ukv[l].reshape(C_KV_LORA, C_HEADS, C_NOPE + C_VDIM)
    wkn = jnp.pad(ukv[..., :C_NOPE], ((0, 0), (0, 0), (0, C_HEAD_PAD - C_NOPE))).reshape(C_KV_LORA, -1)
    wcv = ukv[..., C_NOPE:].reshape(C_KV_LORA, -1)
    g = norms[l]
    row = lambda v: v.reshape(1, -1).astype(F32)
    bf = lambda v: v.astype(BF16)
    return dict(
        wa=bf(wi[:, :na]), wb=bf(wi[:, na:na + nb]), wc=bf(wc), wuq=bf(wuq), eq=bf(eq), ek=bf(ek),
        wkn=bf(wkn), wcv=bf(wcv), g_q=row(mla_q_norm[l]), g_kv=row(mla_kv_norm[l]),
        g_mix_pre=row(g[N_MIX_PRE]), g_mix_post=row(g[N_MIX_POST]), g_x_pre=row(g[N_X_PRE]),
        g_x_post=row(g[N_X_POST]), g_f_pre=row(g[N_F_PRE]), g_f_post=row(g[N_F_POST]), g_mem=row(g[N_MEM]),
        w_out=bf(w_out[l]), w_xq=bf(w_xq[l]), w_mk=bf(w_mk[l]), w_mv=bf(w_mv[l]), w_xo=bf(w_xo[l]),
        w_gate=bf(w_gate[l]), w_up=bf(w_up[l]), w_down=bf(w_down[l]),
        conv_w=conv_w[l].astype(F32), conv_b=row(conv_b[l]))


def _rope_tables(pos):
    half = C_ROPE // 2
    inv = ROPE_THETA ** (-jnp.arange(half, dtype=F32) / half)
    ang = pos.astype(F32)[:, None] * inv[None, :]
    reps = LANES // half
    return jnp.tile(jnp.cos(ang), (1, reps)), jnp.tile(jnp.sin(ang), (1, reps))


def _t5_bucket(rel):
    half = T5_BUCKETS // 2
    exact = half // 2
    n = jnp.abs(rel)
    n2 = jnp.maximum(n * n, 1)
    log2_n2 = 31 - lax.clz(n2.astype(jnp.int32))
    large = jnp.minimum(log2_n2 + 2, half - 1)
    return jnp.where(rel > 0, half, 0) + jnp.where(n < exact, n, large)


def _chunk_causal(qpos, kpos):
    return (kpos[None, :] // CHUNK) <= (qpos[:, None] // CHUNK)


def _t5_bias(t5_table, qpos, kpos, valid):
    bias = jnp.transpose(t5_table[_t5_bucket(kpos[None, :] - qpos[:, None])], (2, 0, 1)).astype(F32)
    return jnp.where((_chunk_causal(qpos, kpos) & valid[None, :])[None], bias, NEG)


def _band_bias(rel_table, qpos, kpos, valid):
    rel = jnp.clip(kpos[None, :] - qpos[:, None], -B_REL_CLIP, B_REL_CLIP) + B_REL_CLIP
    bias = rel_table[:, rel].astype(F32)
    qc = qpos[:, None] // CHUNK
    kc = kpos[None, :] // CHUNK
    mask = (kpos[None, :] >= 0) & (kc <= qc) & (kc >= qc - B_LEFT_CHUNKS) & valid[None, :]
    return jnp.where(mask[None], bias, NEG)


def _causal_bias(qpos, kpos, valid):
    return jnp.where(_chunk_causal(qpos, kpos) & valid[None, :], 0.0, NEG).astype(F32)


def _tail(x, mk16, mv16, past, lw):
    x = _xattn(x, mk16, mv16, lw['g_x_pre'], lw['w_xq'], lw['w_xo'], lw['g_x_post'])
    return _ffn(x, past, lw['g_f_pre'], lw['w_gate'], lw['w_up'], lw['conv_w'], lw['conv_b'], lw['w_down'],
                lw['g_f_post'])


def _layer_prompt(x, mem, lw, lam, subln2, t5_table, rel_table, lam_init):
    B, S, _ = x.shape
    t = ATT_TILE
    pos = jnp.arange(S, dtype=jnp.int32)
    cos4, sin4 = _rope_tables(pos)
    (aq, ak32, ak16, av32, av16, bq, bk32, bk16, bv32, bv16, cq, lat, kpe, kcat, cv) = _inproj(x, lw, cos4, sin4)
    tpos = jnp.arange(t, dtype=jnp.int32)
    ones = jnp.ones((t,), bool)
    bias_a = jnp.stack([_t5_bias(t5_table, tpos, tpos - k * t, ones) for k in range(2)], axis=1)
    cfar = t5_table[_t5_bucket(jnp.int32(-(t + 1)))].astype(F32)
    a = _attn_a(aq, ak16, av16, bias_a, cfar, lam, subln2, lam_init)
    n_back = -(-B_LEFT_CHUNKS * CHUNK // t)
    bias_b = jnp.stack([_band_bias(rel_table, tpos + n_back * t, tpos + (n_back - k) * t, ones)
                        for k in range(n_back + 1)], axis=1)
    b = _attn_b(bq, bk16, bv16, bias_b)
    c = _attn_c(cq, kcat, cv, _causal_bias(tpos, tpos, ones))
    x = _outproj(x, a, b, c, lw['w_out'], lw['g_mix_post'])
    mk32, mk16, mv32, mv16 = _memkv(mem, lw['g_mem'], lw['w_mk'], lw['w_mv'])
    x, conv_state = _tail(x, mk16, mv16, jnp.zeros((B, CONV_W - 1, D_FF), F32), lw)
    nb = min(B_LEFT_CHUNKS * CHUNK, S)
    state = (ak32.reshape(B, S, A_HEADS, 2 * A_HALF), av32.reshape(B, S, A_HEADS, A_VDIM),
             bk32[:, S - nb:].reshape(B, nb, B_HEADS, B_DIM), bv32[:, S - nb:].reshape(B, nb, B_HEADS, B_DIM),
             lat, kpe, mk32.reshape(B, MEM_LEN, M_HEADS, M_DIM), mv32.reshape(B, MEM_LEN, M_HEADS, M_DIM),
             conv_state)
    return x, state


def _pad_keys(cache, new, kp):
    full = jnp.concatenate([cache.astype(BF16), new], axis=1)
    return jnp.pad(full, ((0, 0), (0, kp - full.shape[1]), (0, 0)))


def _layer_sample(x, cache, lw, lam, subln2, t5_table, rel_table, lam_init):
    ak_c, av_c, bk_c, bv_c, lat_c, kpe_c, mk_c, mv_c, conv_c = cache
    B, T, _ = x.shape
    P = ak_c.shape[1]
    nb = bk_c.shape[1]
    pos = P + jnp.arange(T, dtype=jnp.int32)
    cos4, sin4 = _rope_tables(pos)
    (aq, ak32, ak16, av32, av16, bq, bk32, bk16, bv32, bv16, cq, lat, kpe, kcat, cv) = _inproj(x, lw, cos4, sin4)
    kp = -(-(P + T) // ATT_TILE) * ATT_TILE
    kpos = jnp.arange(kp, dtype=jnp.int32)
    valid = kpos < P + T
    a = _dense_attn('a', aq, _pad_keys(ak_c.reshape(B, P, -1), ak16, kp), _pad_keys(av_c.reshape(B, P, -1), av16, kp),
                    _t5_bias(t5_table, pos, kpos, valid), A_HEADS, extra=(lam, subln2), lam_init=lam_init)
    kpb = -(-(nb + T) // LANES) * LANES
    bkpos = (P - nb) + jnp.arange(kpb, dtype=jnp.int32)
    b = _dense_attn('b', bq, _pad_keys(bk_c.reshape(B, nb, -1), bk16, kpb), _pad_keys(bv_c.reshape(B, nb, -1), bv16, kpb),
                    _band_bias(rel_table, pos, bkpos, bkpos < P + T), B_HEADS)
    kcat_c, cv_c = _expand_cache(lat_c, kpe_c, lw)
    mask_c = jnp.broadcast_to(_causal_bias(pos, kpos, valid)[None], (C_HEADS, T, kp))
    c = _dense_attn('c', cq, _pad_keys(kcat_c, kcat, kp), _pad_keys(cv_c, cv, kp), mask_c, C_HEADS)
    x = _outproj(x, a, b, c, lw['w_out'], lw['g_mix_post'])
    x, conv_state = _tail(x, mk_c.reshape(B, MEM_LEN, -1).astype(BF16), mv_c.reshape(B, MEM_LEN, -1).astype(BF16),
                          conv_c.astype(F32), lw)
    state = (ak32.reshape(B, T, A_HEADS, 2 * A_HALF), av32.reshape(B, T, A_HEADS, A_VDIM),
             bk32.reshape(B, T, B_HEADS, B_DIM), bv32.reshape(B, T, B_HEADS, B_DIM), lat, kpe, conv_state)
    return x, state


def kernel(x_prompt, x_sample, cache_a_k, cache_a_v, cache_b_k, cache_b_v, cache_c_latent, cache_c_rope_k, cache_mem_k, cache_mem_v, state_ffn_conv, mem_prompt, w_in, w_out, norms, diff_lambda, diff_subln, t5_bias, band_rel_bias, mla_q_norm, mla_w_uq, mla_kv_norm, mla_w_ukv, w_xq, w_mk, w_mv, w_xo, w_gate, w_up, conv_w, conv_b, w_down):
    xp, xs = x_prompt, x_sample
    sp_all, ss_all = [], []
    for l in range(DEPTH):
        lw = _layer_weights(l, w_in, w_out, norms, mla_q_norm, mla_w_uq, mla_kv_norm, mla_w_ukv, w_xq, w_mk, w_mv,
                            w_xo, w_gate, w_up, conv_w, conv_b, w_down)
        lam_init = 0.8 - 0.6 * math.exp(-0.3 * l)
        lam = diff_lambda[l].astype(F32)
        subln2 = jnp.tile(diff_subln[l].astype(F32).reshape(1, -1), (1, LANES // A_VDIM))
        xp, sp = _layer_prompt(xp, mem_prompt, lw, lam, subln2, t5_bias, band_rel_bias[l], lam_init)
        cache_l = (cache_a_k[l], cache_a_v[l], cache_b_k[l], cache_b_v[l], cache_c_latent[l],
                   cache_c_rope_k[l], cache_mem_k[l], cache_mem_v[l], state_ffn_conv[l])
        xs, ss = _layer_sample(xs, cache_l, lw, lam, subln2, t5_bias, band_rel_bias[l], lam_init)
        sp_all.append(sp)
        ss_all.append(ss)

    def stk(states, i):
        return jnp.stack([s[i] for s in states])

    return (xp, xs) + tuple(stk(sp_all, i) for i in range(9)) + tuple(stk(ss_all, i) for i in range(7))
```

```python
import functools
import math

import jax
import jax.numpy as jnp
import numpy as np
from jax import lax
from jax.experimental import pallas as pl
from jax.experimental.pallas import tpu as pltpu

D_MODEL = 1024
DEPTH = 2
CHUNK = 64
A_HEADS, A_HALF, A_VDIM = 8, 32, 64
B_HEADS, B_DIM, B_LEFT_CHUNKS, B_REL_CLIP = 4, 64, 8, 128
C_HEADS, C_NOPE, C_ROPE, C_VDIM, C_Q_LORA, C_KV_LORA = 4, 64, 32, 64, 256, 128
ROPE_THETA = 10000.0
T5_BUCKETS = 32
T5_MAX_DIST = 128
MEM_LEN, M_HEADS, M_DIM = 256, 4, 64
D_FF = 2816
CONV_W = 3
EPS = 1e-6
NEG = -1e30
N_MIX_PRE, N_MIX_POST, N_X_PRE, N_X_POST, N_F_PRE, N_F_POST, N_MEM = 0, 1, 2, 3, 4, 5, 6

LOG2E = math.log2(math.e)
A_SCALE = A_HALF ** -0.5 * LOG2E
B_SCALE = B_DIM ** -0.5 * LOG2E
C_SCALE = (C_NOPE + C_ROPE) ** -0.5 * LOG2E
M_SCALE = M_DIM ** -0.5

LANES = 128
C_HEAD_PAD = 128
ATT_TILE = 256
FF_CHUNK = 256
VMEM_LIMIT = 56 * 1024 * 1024

BF16 = jnp.bfloat16
F32 = jnp.float32


def _params(n_grid):
    return pltpu.CompilerParams(dimension_semantics=("arbitrary",) * n_grid, vmem_limit_bytes=VMEM_LIMIT)


def _dot(a, b):
    return jnp.dot(a, b, preferred_element_type=F32)


def _dot_nt(a, b):
    return lax.dot_general(a, b, (((1,), (1,)), ((), ())), preferred_element_type=F32)


def _rms(x, g):
    ms = jnp.mean(x * x, axis=-1, keepdims=True)
    return x * lax.rsqrt(ms + EPS) * g


def _full(shape):
    n = len(shape)
    return pl.BlockSpec(shape, lambda *_: (0,) * n)


def _row_tile(rows, cap):
    t = min(rows, cap)
    assert rows % t == 0
    return t


def _mla_expand(lat, kpe, wkn_ref, ek_ref, wcv_ref):
    lat16 = lat.astype(BF16)
    kcat = _dot(lat16, wkn_ref[...]) + _dot(kpe.astype(BF16), ek_ref[...])
    cv = _dot(lat16, wcv_ref[...])
    return kcat.astype(BF16), cv.astype(BF16)


def _inproj_kernel(x_ref, g_ref, cos_ref, sin_ref, wa_ref, wb_ref, wc_ref, gq_ref, wuq_ref, eq_ref,
                   gkv_ref, wkn_ref, ek_ref, wcv_ref,
                   aq_o, ak32_o, ak16_o, av32_o, av16_o, bq_o, bk32_o, bk16_o, bv32_o, bv16_o,
                   cq_o, lat_o, kpe_o, kcat_o, cv_o):
    h = _rms(x_ref[0], g_ref[...]).astype(BF16)
    na = A_HEADS * 2 * A_HALF
    aq_o[0] = (_dot(h, wa_ref[:, 0:na]) * A_SCALE).astype(BF16)
    ak = _dot(h, wa_ref[:, na:2 * na])
    ak32_o[0] = ak
    ak16_o[0] = ak.astype(BF16)
    av = _dot(h, wa_ref[:, 2 * na:3 * na])
    av32_o[0] = av
    av16_o[0] = av.astype(BF16)
    nb = B_HEADS * B_DIM
    bq_o[0] = (_dot(h, wb_ref[:, 0:nb]) * B_SCALE).astype(BF16)
    bk = _dot(h, wb_ref[:, nb:2 * nb])
    bk32_o[0] = bk
    bk16_o[0] = bk.astype(BF16)
    bv = _dot(h, wb_ref[:, 2 * nb:3 * nb])
    bv32_o[0] = bv
    bv16_o[0] = bv.astype(BF16)
    zc = _dot(h, wc_ref[...])
    cos4 = cos_ref[...]
    sin4 = sin_ref[...]
    hq = _rms(zc[:, 0:C_Q_LORA], gq_ref[...]).astype(BF16)
    nq = C_HEADS * C_HEAD_PAD
    zq = _dot(hq, wuq_ref[...])
    qpe = (zq[:, nq:nq + LANES] * cos4 + zq[:, nq + LANES:nq + 2 * LANES] * sin4) * C_SCALE
    cq_o[0] = (zq[:, 0:nq] * C_SCALE + _dot(qpe.astype(BF16), eq_ref[...])).astype(BF16)
    lat = _rms(zc[:, C_Q_LORA:C_Q_LORA + C_KV_LORA], gkv_ref[...])
    lat_o[0] = lat
    o = C_Q_LORA + C_KV_LORA
    kpe = zc[:, o:o + C_ROPE] * cos4[:, 0:C_ROPE] + zc[:, o + C_ROPE:o + 2 * C_ROPE] * sin4[:, 0:C_ROPE]
    kpe_o[0] = kpe
    kcat, cv = _mla_expand(lat, kpe, wkn_ref, ek_ref, wcv_ref)
    kcat_o[0] = kcat
    cv_o[0] = cv


def _inproj(x, lw, cos4, sin4):
    B, S, _ = x.shape
    tm = _row_tile(S, 512)
    grid = (B, S // tm)
    row = lambda w: pl.BlockSpec((1, tm, w), lambda b, i: (b, i, 0))
    pos = pl.BlockSpec((tm, LANES), lambda b, i: (i, 0))
    weights = [lw['g_mix_pre'], cos4, sin4, lw['wa'], lw['wb'], lw['wc'], lw['g_q'], lw['wuq'], lw['eq'],
               lw['g_kv'], lw['wkn'], lw['ek'], lw['wcv']]
    in_specs = [row(D_MODEL)] + [pos if w is cos4 or w is sin4 else _full(w.shape) for w in weights]
    widths = [(512, BF16), (512, F32), (512, BF16), (512, F32), (512, BF16),
              (256, BF16), (256, F32), (256, BF16), (256, F32), (256, BF16),
              (C_HEADS * C_HEAD_PAD, BF16), (C_KV_LORA, F32), (C_ROPE, F32),
              (C_HEADS * C_HEAD_PAD, BF16), (C_HEADS * C_VDIM, BF16)]
    return pl.pallas_call(
        _inproj_kernel, grid=grid, in_specs=in_specs,
        out_specs=[row(w) for w, _ in widths],
        out_shape=[jax.ShapeDtypeStruct((B, S, w), d) for w, d in widths],
        compiler_params=_params(2), name="inproj")(x, *weights)


def _expand_kernel(lat_ref, kpe_ref, wkn_ref, ek_ref, wcv_ref, kcat_o, cv_o):
    kcat, cv = _mla_expand(lat_ref[0], kpe_ref[0], wkn_ref, ek_ref, wcv_ref)
    kcat_o[0] = kcat
    cv_o[0] = cv


def _expand_cache(lat, kpe, lw):
    B, P, _ = lat.shape
    tm = _row_tile(P, 512)
    row = lambda w: pl.BlockSpec((1, tm, w), lambda b, i: (b, i, 0))
    ws = [lw['wkn'], lw['ek'], lw['wcv']]
    return pl.pallas_call(
        _expand_kernel, grid=(B, P // tm),
        in_specs=[row(C_KV_LORA), row(C_ROPE)] + [_full(w.shape) for w in ws],
        out_specs=[row(C_HEADS * C_HEAD_PAD), row(C_HEADS * C_VDIM)],
        out_shape=[jax.ShapeDtypeStruct((B, P, C_HEADS * C_HEAD_PAD), BF16),
                   jax.ShapeDtypeStruct((B, P, C_HEADS * C_VDIM), BF16)],
        compiler_params=_params(2), name="mla_expand")(lat, kpe, *ws)


def _lane_iota():
    return lax.broadcasted_iota(jnp.int32, (1, LANES), 1)


def _masked_streams(q, width):
    lane = _lane_iota()
    return [jnp.where(lane // width == s, q, jnp.zeros_like(q)) for s in range(LANES // width)]


def _with_ones(vt):
    first = _lane_iota() < 64
    one = jnp.ones_like(vt)
    return [jnp.where(first, vt, one), jnp.where(first, one, vt)]


def _normalise(acc):
    return acc / pltpu.roll(acc, 64, axis=1)


def _online_update(scs, vts, m_ref, acc_ref):
    reps = scs[0].shape[1] // LANES
    ps, alphas = [], []
    for s, sc in enumerate(scs):
        m_prev = m_ref[s]
        m_next = jnp.maximum(m_prev, jnp.max(sc, axis=1, keepdims=True))
        alphas.append(jnp.exp2(m_prev - m_next))
        ps.append(jnp.exp2(sc - jnp.concatenate([m_next] * reps, axis=1)).astype(BF16))
        m_ref[s] = m_next
    for s, (p, alpha, vt1) in enumerate(zip(ps, alphas, vts)):
        acc_ref[s] = alpha * acc_ref[s] + _dot(p, vt1)


def _apply_bias(sc, bt):
    return jnp.where(bt > 0.5 * NEG, sc + bt, NEG)


def _softmax_pv(sc, vt1):
    m = jnp.max(sc, axis=1, keepdims=True)
    p = jnp.exp2(sc - m).astype(BF16)
    return _normalise(_dot(p, vt1))


def _init_state(m_ref, acc_ref):
    m_ref[...] = jnp.full(m_ref.shape, NEG, F32)
    acc_ref[...] = jnp.zeros(acc_ref.shape, F32)


def _diff_lambda(lam_ref, lam_init):
    lf = lam_ref[...]
    return (jnp.exp(jnp.sum(lf[0:1] * lf[1:2], axis=1, keepdims=True))
            - jnp.exp(jnp.sum(lf[2:3] * lf[3:4], axis=1, keepdims=True)) + lam_init)


def _finish_diff(outs, lam, subln_ref, lam_init):
    lane = _lane_iota()
    first = lane < A_VDIM
    o = jnp.where(first, outs[0] - lam * outs[1], outs[2] - lam * outs[3])
    sq = o * o
    ss0 = jnp.sum(jnp.where(first, sq, 0.0), axis=1, keepdims=True)
    ss1 = jnp.sum(jnp.where(first, 0.0, sq), axis=1, keepdims=True)
    ms = jnp.where(first, ss0, ss1) * (1.0 / A_VDIM)
    return (o * lax.rsqrt(ms + EPS) * subln_ref[...]) * (1.0 - lam_init)


def _attn_a_kernel(lam_ref, subln_ref, q_ref, k_ref, v_ref, bias_ref, o_ref, m_ref, acc_ref, *, lam_init, t):
    i = pl.program_id(2)
    qs = _masked_streams(q_ref[0], A_HALF)
    _init_state(m_ref, acc_ref)

    def tile(j, kind):
        start = pl.multiple_of(j * t, t)
        kt = k_ref[0, pl.ds(start, t), :]
        vt1 = _with_ones(v_ref[0, pl.ds(start, t), :])
        scs = [_dot_nt(qs[s], kt) for s in range(4)]
        if kind == 'near':
            scs = [sc + bias_ref[s // 2, 1] for s, sc in enumerate(scs)]
        elif kind == 'diag':
            scs = [_apply_bias(sc, bias_ref[s // 2, 0]) for s, sc in enumerate(scs)]
        _online_update(scs, [vt1[s // 2] for s in range(4)], m_ref, acc_ref)

    tile(i, 'diag')

    @pl.when(i >= 1)
    def _():
        tile(i - 1, 'near')

    def far_body(j, carry):
        tile(j, 'far')
        return carry

    lax.fori_loop(0, jnp.maximum(i - 1, 0), far_body, 0)
    outs = [_normalise(acc_ref[s]) for s in range(4)]
    o_ref[0] = _finish_diff(outs, _diff_lambda(lam_ref, lam_init), subln_ref, lam_init).astype(BF16)


def _attn_a(aq, ak, av, bias, lam, subln2, lam_init):
    B, S, _ = aq.shape
    t = ATT_TILE
    assert S % t == 0
    kernel = functools.partial(_attn_a_kernel, lam_init=lam_init, t=t)
    return pl.pallas_call(
        kernel, grid=(B, A_HEADS // 2, S // t),
        in_specs=[_full(lam.shape), _full(subln2.shape),
                  pl.BlockSpec((1, t, LANES), lambda b, p, i: (b, i, p)),
                  pl.BlockSpec((1, S, LANES), lambda b, p, i: (b, 0, p)),
                  pl.BlockSpec((1, S, LANES), lambda b, p, i: (b, 0, p)),
                  pl.BlockSpec((2, 2, t, t), lambda b, p, i: (p, 0, 0, 0))],
        out_specs=pl.BlockSpec((1, t, LANES), lambda b, p, i: (b, i, p)),
        out_shape=jax.ShapeDtypeStruct((B, S, A_HEADS * A_VDIM), BF16),
        scratch_shapes=[pltpu.VMEM((4, t, LANES), F32)] * 2,
        compiler_params=_params(3), name="attn_diff")(lam, subln2, aq, ak, av, bias)


def _attn_b_kernel(q_ref, k_ref, v_ref, bias_ref, o_ref, m_ref, acc_ref, *, t, n_back):
    i = pl.program_id(2)
    qs = _masked_streams(q_ref[0], B_DIM)
    _init_state(m_ref, acc_ref)

    def tile(back):
        start = pl.multiple_of((i - back) * t, t)
        kt = k_ref[0, pl.ds(start, t), :]
        vt1 = _with_ones(v_ref[0, pl.ds(start, t), :])
        scs = [_apply_bias(_dot_nt(qs[s], kt), bias_ref[s, back]) for s in range(2)]
        _online_update(scs, vt1, m_ref, acc_ref)

    tile(0)
    for back in range(1, n_back + 1):
        pl.when(i >= back)(functools.partial(tile, back))
    o_ref[0] = jnp.where(_lane_iota() < B_DIM, _normalise(acc_ref[0]), _normalise(acc_ref[1])).astype(BF16)


def _attn_b(bq, bk, bv, bias):
    B, S, _ = bq.shape
    t = ATT_TILE
    n_back = bias.shape[1] - 1
    assert S % t == 0
    kernel = functools.partial(_attn_b_kernel, t=t, n_back=n_back)
    return pl.pallas_call(
        kernel, grid=(B, B_HEADS // 2, S // t),
        in_specs=[pl.BlockSpec((1, t, LANES), lambda b, p, i: (b, i, p)),
                  pl.BlockSpec((1, S, LANES), lambda b, p, i: (b, 0, p)),
                  pl.BlockSpec((1, S, LANES), lambda b, p, i: (b, 0, p)),
                  pl.BlockSpec((2, n_back + 1, t, t), lambda b, p, i: (p, 0, 0, 0))],
        out_specs=pl.BlockSpec((1, t, LANES), lambda b, p, i: (b, i, p)),
        out_shape=jax.ShapeDtypeStruct((B, S, B_HEADS * B_DIM), BF16),
        scratch_shapes=[pltpu.VMEM((2, t, LANES), F32)] * 2,
        compiler_params=_params(3), name="attn_band")(bq, bk, bv, bias)


def _attn_c_kernel(q_ref, k_ref, v_ref, mask_ref, o_ref, m_ref, acc_ref, *, t):
    i = pl.program_id(2)
    q = q_ref[0]
    qs = [q[:, 0:C_HEAD_PAD], q[:, C_HEAD_PAD:2 * C_HEAD_PAD]]
    _init_state(m_ref, acc_ref)

    def tile(j, diag):
        start = pl.multiple_of(j * t, t)
        kt = k_ref[0, pl.ds(start, t), :]
        vt1 = _with_ones(v_ref[0, pl.ds(start, t), :])
        scs = [_dot_nt(qs[s], kt[:, s * C_HEAD_PAD:(s + 1) * C_HEAD_PAD]) for s in range(2)]
        if diag:
            scs = [_apply_bias(sc, mask_ref[...]) for sc in scs]
        _online_update(scs, vt1, m_ref, acc_ref)

    tile(i, True)

    def far_body(j, carry):
        tile(j, False)
        return carry

    lax.fori_loop(0, i, far_body, 0)
    o_ref[0] = jnp.where(_lane_iota() < C_VDIM, _normalise(acc_ref[0]), _normalise(acc_ref[1])).astype(BF16)


def _attn_c(cq, kcat, cv, mask):
    B, S, _ = cq.shape
    t = ATT_TILE
    assert S % t == 0
    kernel = functools.partial(_attn_c_kernel, t=t)
    return pl.pallas_call(
        kernel, grid=(B, C_HEADS // 2, S // t),
        in_specs=[pl.BlockSpec((1, t, 2 * C_HEAD_PAD), lambda b, p, i: (b, i, p)),
                  pl.BlockSpec((1, S, 2 * C_HEAD_PAD), lambda b, p, i: (b, 0, p)),
                  pl.BlockSpec((1, S, LANES), lambda b, p, i: (b, 0, p)),
                  _full(mask.shape)],
        out_specs=pl.BlockSpec((1, t, LANES), lambda b, p, i: (b, i, p)),
        out_shape=jax.ShapeDtypeStruct((B, S, C_HEADS * C_VDIM), BF16),
        scratch_shapes=[pltpu.VMEM((2, t, LANES), F32)] * 2,
        compiler_params=_params(3), name="attn_mla")(cq, kcat, cv, mask)


def _dense_attn_kernel(*refs, mode, lam_init):
    if mode == 'a':
        lam_ref, subln_ref, q_ref, k_ref, v_ref, bias_ref, o_ref = refs
    else:
        q_ref, k_ref, v_ref, bias_ref, o_ref = refs
    q = q_ref[0]
    k = k_ref[0]
    vt1 = _with_ones(v_ref[0])
    if mode == 'a':
        streams = [(qm, k, s // 2) for s, qm in enumerate(_masked_streams(q, A_HALF))]
    elif mode == 'b':
        streams = [(qm, k, s) for s, qm in enumerate(_masked_streams(q, B_DIM))]
    else:
        streams = [(q[:, s * C_HEAD_PAD:(s + 1) * C_HEAD_PAD], k[:, s * C_HEAD_PAD:(s + 1) * C_HEAD_PAD], s)
                   for s in range(2)]
    outs = [_softmax_pv(_apply_bias(_dot_nt(qm, km), bias_ref[hh]), vt1[hh]) for qm, km, hh in streams]
    if mode == 'a':
        o = _finish_diff(outs, _diff_lambda(lam_ref, lam_init), subln_ref, lam_init)
    else:
        o = jnp.where(_lane_iota() < 64, outs[0], outs[1])
    o_ref[0] = o.astype(BF16)


def _dense_attn(mode, q, k, v, bias, n_heads, extra=(), lam_init=0.0):
    B, T, _ = q.shape
    Kp = k.shape[1]
    qw = 2 * C_HEAD_PAD if mode == 'c' else LANES
    kernel = functools.partial(_dense_attn_kernel, mode=mode, lam_init=lam_init)
    return pl.pallas_call(
        kernel, grid=(B, n_heads // 2),
        in_specs=[_full(e.shape) for e in extra] + [
            pl.BlockSpec((1, T, qw), lambda b, p: (b, 0, p)),
            pl.BlockSpec((1, Kp, qw), lambda b, p: (b, 0, p)),
            pl.BlockSpec((1, Kp, LANES), lambda b, p: (b, 0, p)),
            pl.BlockSpec((2, T, Kp), lambda b, p: (p, 0, 0))],
        out_specs=pl.BlockSpec((1, T, LANES), lambda b, p: (b, 0, p)),
        out_shape=jax.ShapeDtypeStruct((B, T, n_heads * 64), BF16),
        compiler_params=_params(2), name="attn_dense_" + mode)(*extra, q, k, v, bias)


def _outproj_kernel(x_ref, a_ref, b_ref, c_ref, w_ref, g_ref, o_ref):
    na = A_HEADS * A_VDIM
    nb = na + B_HEADS * B_DIM
    y = (_dot(a_ref[0], w_ref[0:na, :]) + _dot(b_ref[0], w_ref[na:nb, :]) + _dot(c_ref[0], w_ref[nb:D_MODEL, :]))
    o_ref[0] = x_ref[0] + _rms(y, g_ref[...])


def _outproj(x, a, b, c, w, g):
    B, S, _ = x.shape
    tm = _row_tile(S, 512)
    row = lambda wd: pl.BlockSpec((1, tm, wd), lambda bb, i: (bb, i, 0))
    return pl.pallas_call(
        _outproj_kernel, grid=(B, S // tm),
        in_specs=[row(D_MODEL), row(a.shape[2]), row(b.shape[2]), row(c.shape[2]), _full(w.shape), _full(g.shape)],
        out_specs=row(D_MODEL), out_shape=jax.ShapeDtypeStruct(x.shape, F32),
        compiler_params=_params(2), name="outproj")(x, a, b, c, w, g)


def _memkv_kernel(mem_ref, g_ref, wk_ref, wv_ref, k32_o, k16_o, v32_o, v16_o):
    m = _rms(mem_ref[0], g_ref[...]).astype(BF16)
    mk = _dot(m, wk_ref[...])
    mv = _dot(m, wv_ref[...])
    k32_o[0] = mk
    k16_o[0] = mk.astype(BF16)
    v32_o[0] = mv
    v16_o[0] = mv.astype(BF16)


def _memkv(mem, g, wk, wv):
    B = mem.shape[0]
    n = M_HEADS * M_DIM
    blk = pl.BlockSpec((1, MEM_LEN, n), lambda b: (b, 0, 0))
    return pl.pallas_call(
        _memkv_kernel, grid=(B,),
        in_specs=[pl.BlockSpec((1, MEM_LEN, D_MODEL), lambda b: (b, 0, 0)), _full(g.shape), _full(wk.shape),
                  _full(wv.shape)],
        out_specs=[blk] * 4,
        out_shape=[jax.ShapeDtypeStruct((B, MEM_LEN, n), d) for d in (F32, BF16, F32, BF16)],
        compiler_params=_params(1), name="mem_kv")(mem, g, wk, wv)


def _xattn_kernel(x_ref, mk_ref, mv_ref, gpre_ref, wq_ref, wo_ref, gpost_ref, o_ref):
    x = x_ref[0]
    h = _rms(x, gpre_ref[...]).astype(BF16)
    q = (_dot(h, wq_ref[...]) * (M_SCALE * LOG2E)).astype(BF16)
    lane = _lane_iota()
    pairs = []
    for p in range(M_HEADS // 2):
        sl = slice(p * LANES, (p + 1) * LANES)
        kp = mk_ref[0, :, sl]
        vp1 = _with_ones(mv_ref[0, :, sl])
        outs = [_softmax_pv(_dot_nt(qm, kp), vp1[s]) for s, qm in enumerate(_masked_streams(q[:, sl], M_DIM))]
        pairs.append(jnp.where(lane < M_DIM, outs[0], outs[1]))
    o = jnp.concatenate(pairs, axis=1).astype(BF16)
    o_ref[0] = x + _rms(_dot(o, wo_ref[...]), gpost_ref[...])


def _xattn(x, mk, mv, gpre, wq, wo, gpost):
    B, S, _ = x.shape
    tm = _row_tile(S, 512)
    n = M_HEADS * M_DIM
    row = pl.BlockSpec((1, tm, D_MODEL), lambda b, i: (b, i, 0))
    mem = pl.BlockSpec((1, MEM_LEN, n), lambda b, i: (b, 0, 0))
    return pl.pallas_call(
        _xattn_kernel, grid=(B, S // tm),
        in_specs=[row, mem, mem, _full(gpre.shape), _full(wq.shape), _full(wo.shape), _full(gpost.shape)],
        out_specs=row, out_shape=jax.ShapeDtypeStruct(x.shape, F32),
        compiler_params=_params(2), name="xattn")(x, mk, mv, gpre, wq, wo, gpost)


def _ffn_kernel(x_ref, halo_ref, past_ref, gpre_ref, wg_ref, wu_ref, cw_ref, cb_ref, wd_ref, gpost_ref,
                o_ref, st_ref, *, tm):
    i = pl.program_id(1)
    x = x_ref[0]
    gpre = gpre_ref[...]
    h = _rms(x, gpre).astype(BF16)
    hh = _rms(halo_ref[0], gpre).astype(BF16)
    row = lax.broadcasted_iota(jnp.int32, (tm, 1), 0)
    first = i == 0
    acc = jnp.zeros((tm, D_MODEL), F32)
    for c in range(D_FF // FF_CHUNK):
        sl = slice(c * FF_CHUNK, (c + 1) * FF_CHUNK)
        wg = wg_ref[:, sl]
        g = _dot(h, wg)
        gh = _dot(hh, wg)
        past = past_ref[0, :, sl]
        p1 = jnp.where(first, past[1:2], gh[7:8])
        p2 = jnp.where(first, past[0:1], gh[6:7])
        g1 = jnp.where(row == 0, p1, pltpu.roll(g, 1, axis=0))
        g2 = jnp.where(row == 0, p2, jnp.where(row == 1, p1, pltpu.roll(g, 2, axis=0)))
        cw = cw_ref[:, sl]
        cc = cb_ref[:, sl] + g2 * cw[0:1] + g1 * cw[1:2] + g * cw[2:3]
        u = _dot(h, wu_ref[:, sl])
        a = (cc * jax.nn.sigmoid(cc) * u).astype(BF16)
        acc = acc + _dot(a, wd_ref[sl, :])
        st_ref[0, :, sl] = g[tm - 2:tm]
    o_ref[0] = x + _rms(acc, gpost_ref[...])


def _ffn(x, past, gpre, wg, wu, cw, cb, wd, gpost):
    B, S, _ = x.shape
    tm = _row_tile(S, 512)
    assert tm % 8 == 0
    hb = tm // 8
    row = pl.BlockSpec((1, tm, D_MODEL), lambda b, i: (b, i, 0))
    halo = pl.BlockSpec((1, 8, D_MODEL), lambda b, i: (b, jnp.maximum(i * hb - 1, 0), 0))
    st = pl.BlockSpec((1, CONV_W - 1, D_FF), lambda b, i: (b, 0, 0))
    ws = [gpre, wg, wu, cw, cb, wd, gpost]
    return pl.pallas_call(
        functools.partial(_ffn_kernel, tm=tm), grid=(B, S // tm),
        in_specs=[row, halo, st] + [_full(w.shape) for w in ws],
        out_specs=[row, st],
        out_shape=[jax.ShapeDtypeStruct(x.shape, F32), jax.ShapeDtypeStruct((B, CONV_W - 1, D_FF), F32)],
        compiler_params=_params(2), name="ffn")(x, x, past, *ws)


def _rot_cols(w):
    half = C_ROPE // 2
    return jnp.concatenate([-w[..., half:], w[..., :half]], axis=-1)


def _layer_weights(l, w_in, w_out, norms, mla_q_norm, mla_w_uq, mla_kv_norm, mla_w_ukv, w_xq, w_mk, w_mv, w_xo,
                   w_gate, w_up, conv_w, conv_b, w_down):
    wi = w_in[l]
    na, nb = 3 * A_HEADS * 2 * A_HALF, 3 * B_HEADS * B_DIM
    wcq = wi[:, na + nb:na + nb + C_Q_LORA]
    wckv = wi[:, na + nb + C_Q_LORA:na + nb + C_Q_LORA + C_KV_LORA]
    wcpe = wi[:, na + nb + C_Q_LORA + C_KV_LORA:]
    wc = jnp.concatenate([wcq, wckv, wcpe, _rot_cols(wcpe),
                          jnp.zeros((D_MODEL, 512 - C_Q_LORA - C_KV_LORA - 2 * C_ROPE), F32)], axis=1)
    uq = mla_w_uq[l].reshape(C_Q_LORA, C_HEADS, C_NOPE + C_ROPE)
    uq_n = jnp.pad(uq[..., :C_NOPE], ((0, 0), (0, 0), (0, C_HEAD_PAD - C_NOPE))).reshape(C_Q_LORA, -1)
    uq_p = uq[..., C_NOPE:]
    wuq = jnp.concatenate([uq_n, uq_p.reshape(C_Q_LORA, -1), _rot_cols(uq_p).reshape(C_Q_LORA, -1)], axis=1)
    eye = jnp.eye(C_ROPE, dtype=F32)
    place = jnp.pad(eye, ((0, 0), (C_NOPE, C_HEAD_PAD - C_NOPE - C_ROPE)))
    eq = jnp.kron(jnp.eye(C_HEADS, dtype=F32), place)
    ek = jnp.tile(place, (1, C_HEADS))
    ukv = mla_w_ukv[l].reshape(C_KV_LORA, C_HEADS, C_NOPE + C_VDIM)
    wkn = jnp.pad(ukv[..., :C_NOPE], ((0, 0), (0, 0), (0, C_HEAD_PAD - C_NOPE))).reshape(C_KV_LORA, -1)
    wcv = ukv[..., C_NOPE:].reshape(C_KV_LORA, -1)
    g = norms[l]
    row = lambda v: v.reshape(1, -1).astype(F32)
    bf = lambda v: v.astype(BF16)
    return dict(
        wa=bf(wi[:, :na]), wb=bf(wi[:, na:na + nb]), wc=bf(wc), wuq=bf(wuq), eq=bf(eq), ek=bf(ek),
        wkn=bf(wkn), wcv=bf(wcv), g_q=row(mla_q_norm[l]), g_kv=row(mla_kv_norm[l]),
        g_mix_pre=row(g[N_MIX_PRE]), g_mix_post=row(g[N_MIX_POST]), g_x_pre=row(g[N_X_PRE]),
        g_x_post=row(g[N_X_POST]), g_f_pre=row(g[N_F_PRE]), g_f_post=row(g[N_F_POST]), g_mem=row(g[N_MEM]),
        w_out=bf(w_out[l]), w_xq=bf(w_xq[l]), w_mk=bf(w_mk[l]), w_mv=bf(w_mv[l]), w_xo=bf(w_xo[l]),
        w_gate=bf(w_gate[l]), w_up=bf(w_up[l]), w_down=bf(w_down[l]),
        conv_w=conv_w[l].astype(F32), conv_b=row(conv_b[l]))


def _rope_tables(pos):
    half = C_ROPE // 2
    inv = ROPE_THETA ** (-jnp.arange(half, dtype=F32) / half)
    ang = pos.astype(F32)[:, None] * inv[None, :]
    reps = LANES // half
    return jnp.tile(jnp.cos(ang), (1, reps)), jnp.tile(jnp.sin(ang), (1, reps))


def _t5_bucket(rel):
    half = T5_BUCKETS // 2
    exact = half // 2
    n = jnp.abs(rel)
    n2 = jnp.maximum(n * n, 1)
    log2_n2 = 31 - lax.clz(n2.astype(jnp.int32))
    large = jnp.minimum(log2_n2 + 2, half - 1)
    return jnp.where(rel > 0, half, 0) + jnp.where(n < exact, n, large)


def _chunk_causal(qpos, kpos):
    return (kpos[None, :] // CHUNK) <= (qpos[:, None] // CHUNK)


def _t5_lookup(t5_table):
    return lambda rel: t5_table[_t5_bucket(rel)].astype(F32).T


def _band_lookup(rel_table):
    return lambda rel: rel_table[:, jnp.clip(rel, -B_REL_CLIP, B_REL_CLIP) + B_REL_CLIP].astype(F32)


def _toeplitz_tile(lookup, t, d):
    m = jnp.arange(2 * t, dtype=jnp.int32)
    v = lookup(d + m - 2 * t * (m >= t))
    h = v.shape[0]
    strip = jnp.broadcast_to(v[:, None, :], (h, t, 2 * t)).reshape(h, 2 * t * t)
    return strip[:, :t * (2 * t - 1)].reshape(h, t, 2 * t - 1)[:, :, :t]


def _toeplitz_rows(lookup, q0, nq, k0, nk):
    w = lookup(k0 - q0 - (nq - 1) + jnp.arange(nk + nq - 1, dtype=jnp.int32))
    return jnp.stack([w[:, nq - 1 - i:nq - 1 - i + nk] for i in range(nq)], axis=1)


def _band_mask(qpos, kpos):
    qc = qpos[:, None] // CHUNK
    kc = kpos[None, :] // CHUNK
    return (kpos[None, :] >= 0) & (kc <= qc) & (kc >= qc - B_LEFT_CHUNKS)


def _masked(bias, mask):
    return jnp.where(mask, bias * LOG2E, NEG)


def _tail(x, mk16, mv16, past, lw):
    x = _xattn(x, mk16, mv16, lw['g_x_pre'], lw['w_xq'], lw['w_xo'], lw['g_x_post'])
    return _ffn(x, past, lw['g_f_pre'], lw['w_gate'], lw['w_up'], lw['conv_w'], lw['conv_b'], lw['w_down'],
                lw['g_f_post'])


def _layer_prompt(x, mem, lw, lam, subln2, t5_table, rel_table, lam_init):
    B, S, _ = x.shape
    t = ATT_TILE
    pos = jnp.arange(S, dtype=jnp.int32)
    cos4, sin4 = _rope_tables(pos)
    (aq, ak32, ak16, av32, av16, bq, bk32, bk16, bv32, bv16, cq, lat, kpe, kcat, cv) = _inproj(x, lw, cos4, sin4)
    tpos = jnp.arange(t, dtype=jnp.int32)
    causal = _chunk_causal(tpos, tpos)
    assert t + 1 >= T5_MAX_DIST
    t5 = _t5_lookup(t5_table)
    far = t5(jnp.full((1,), -(t + 1), jnp.int32))[:, :, None]
    bias_a = jnp.stack([_masked(_toeplitz_tile(t5, t, 0) - far, causal[None]),
                        (_toeplitz_tile(t5, t, -t) - far) * LOG2E], axis=1)
    a = _attn_a(aq, ak16, av16, bias_a, lam, subln2, lam_init)
    n_back = -(-B_LEFT_CHUNKS * CHUNK // t)
    band = _band_lookup(rel_table)
    bias_b = jnp.stack([_masked(_toeplitz_tile(band, t, -k * t),
                                _band_mask(tpos + n_back * t, tpos + (n_back - k) * t)[None])
                        for k in range(n_back + 1)], axis=1)
    b = _attn_b(bq, bk16, bv16, bias_b)
    c = _attn_c(cq, kcat, cv, _masked(jnp.zeros((t, t), F32), causal))
    x = _outproj(x, a, b, c, lw['w_out'], lw['g_mix_post'])
    mk32, mk16, mv32, mv16 = _memkv(mem, lw['g_mem'], lw['w_mk'], lw['w_mv'])
    x, conv_state = _tail(x, mk16, mv16, jnp.zeros((B, CONV_W - 1, D_FF), F32), lw)
    nb = min(B_LEFT_CHUNKS * CHUNK, S)
    state = (ak32.reshape(B, S, A_HEADS, 2 * A_HALF), av32.reshape(B, S, A_HEADS, A_VDIM),
             bk32[:, S - nb:].reshape(B, nb, B_HEADS, B_DIM), bv32[:, S - nb:].reshape(B, nb, B_HEADS, B_DIM),
             lat, kpe, mk32.reshape(B, MEM_LEN, M_HEADS, M_DIM), mv32.reshape(B, MEM_LEN, M_HEADS, M_DIM),
             conv_state)
    return x, state


def _pad_keys(cache, new, kp):
    full = jnp.concatenate([cache.astype(BF16), new], axis=1)
    return jnp.pad(full, ((0, 0), (0, kp - full.shape[1]), (0, 0)))


def _layer_sample(x, cache, lw, lam, subln2, t5_table, rel_table, lam_init):
    ak_c, av_c, bk_c, bv_c, lat_c, kpe_c, mk_c, mv_c, conv_c = cache
    B, T, _ = x.shape
    P = ak_c.shape[1]
    nb = bk_c.shape[1]
    pos = P + jnp.arange(T, dtype=jnp.int32)
    cos4, sin4 = _rope_tables(pos)
    (aq, ak32, ak16, av32, av16, bq, bk32, bk16, bv32, bv16, cq, lat, kpe, kcat, cv) = _inproj(x, lw, cos4, sin4)
    kp = -(-(P + T) // ATT_TILE) * ATT_TILE
    kpos = jnp.arange(kp, dtype=jnp.int32)
    valid = kpos < P + T
    causal = _chunk_causal(pos, kpos) & valid[None, :]
    bias_a = _masked(_toeplitz_rows(_t5_lookup(t5_table), P, T, 0, kp), causal[None])
    a = _dense_attn('a', aq, _pad_keys(ak_c.reshape(B, P, -1), ak16, kp), _pad_keys(av_c.reshape(B, P, -1), av16, kp),
                    bias_a, A_HEADS, extra=(lam, subln2), lam_init=lam_init)
    kpb = -(-(nb + T) // LANES) * LANES
    bkpos = (P - nb) + jnp.arange(kpb, dtype=jnp.int32)
    bias_b = _masked(_toeplitz_rows(_band_lookup(rel_table), P, T, P - nb, kpb),
                     (_band_mask(pos, bkpos) & (bkpos < P + T)[None, :])[None])
    b = _dense_attn('b', bq, _pad_keys(bk_c.reshape(B, nb, -1), bk16, kpb), _pad_keys(bv_c.reshape(B, nb, -1), bv16, kpb),
                    bias_b, B_HEADS)
    kcat_c, cv_c = _expand_cache(lat_c, kpe_c, lw)
    mask_c = jnp.broadcast_to(_masked(jnp.zeros((T, kp), F32), causal)[None], (C_HEADS, T, kp))
    c = _dense_attn('c', cq, _pad_keys(kcat_c, kcat, kp), _pad_keys(cv_c, cv, kp), mask_c, C_HEADS)
    x = _outproj(x, a, b, c, lw['w_out'], lw['g_mix_post'])
    x, conv_state = _tail(x, mk_c.reshape(B, MEM_LEN, -1).astype(BF16), mv_c.reshape(B, MEM_LEN, -1).astype(BF16),
                          conv_c.astype(F32), lw)
    state = (ak32.reshape(B, T, A_HEADS, 2 * A_HALF), av32.reshape(B, T, A_HEADS, A_VDIM),
             bk32.reshape(B, T, B_HEADS, B_DIM), bv32.reshape(B, T, B_HEADS, B_DIM), lat, kpe, conv_state)
    return x, state


def kernel(x_prompt, x_sample, cache_a_k, cache_a_v, cache_b_k, cache_b_v, cache_c_latent, cache_c_rope_k, cache_mem_k, cache_mem_v, state_ffn_conv, mem_prompt, w_in, w_out, norms, diff_lambda, diff_subln, t5_bias, band_rel_bias, mla_q_norm, mla_w_uq, mla_kv_norm, mla_w_ukv, w_xq, w_mk, w_mv, w_xo, w_gate, w_up, conv_w, conv_b, w_down):
    xp, xs = x_prompt, x_sample
    sp_all, ss_all = [], []
    for l in range(DEPTH):
        lw = _layer_weights(l, w_in, w_out, norms, mla_q_norm, mla_w_uq, mla_kv_norm, mla_w_ukv, w_xq, w_mk, w_mv,
                            w_xo, w_gate, w_up, conv_w, conv_b, w_down)
        lam_init = 0.8 - 0.6 * math.exp(-0.3 * l)
        lam = diff_lambda[l].astype(F32)
        subln2 = jnp.tile(diff_subln[l].astype(F32).reshape(1, -1), (1, LANES // A_VDIM))
        xp, sp = _layer_prompt(xp, mem_prompt, lw, lam, subln2, t5_bias, band_rel_bias[l], lam_init)
        cache_l = (cache_a_k[l], cache_a_v[l], cache_b_k[l], cache_b_v[l], cache_c_latent[l],
                   cache_c_rope_k[l], cache_mem_k[l], cache_mem_v[l], state_ffn_conv[l])
        xs, ss = _layer_sample(xs, cache_l, lw, lam, subln2, t5_bias, band_rel_bias[l], lam_init)
        sp_all.append(sp)
        ss_all.append(ss)

    def stk(states, i):
        return jnp.stack([s[i] for s in states])

    return (xp, xs) + tuple(stk(sp_all, i) for i in range(9)) + tuple(stk(ss_all, i) for i in range(7))
```

```python
import functools
import math

import jax
import jax.numpy as jnp
import numpy as np
from jax import lax
from jax.experimental import pallas as pl
from jax.experimental.pallas import tpu as pltpu

D_MODEL = 1024
DEPTH = 2
CHUNK = 64
A_HEADS, A_HALF, A_VDIM = 8, 32, 64
B_HEADS, B_DIM, B_LEFT_CHUNKS, B_REL_CLIP = 4, 64, 8, 128
C_HEADS, C_NOPE, C_ROPE, C_VDIM, C_Q_LORA, C_KV_LORA = 4, 64, 32, 64, 256, 128
ROPE_THETA = 10000.0
T5_BUCKETS = 32
T5_MAX_DIST = 128
MEM_LEN, M_HEADS, M_DIM = 256, 4, 64
D_FF = 2816
CONV_W = 3
EPS = 1e-6
NEG = -1e30
N_MIX_PRE, N_MIX_POST, N_X_PRE, N_X_POST, N_F_PRE, N_F_POST, N_MEM = 0, 1, 2, 3, 4, 5, 6

LOG2E = math.log2(math.e)
A_SCALE = A_HALF ** -0.5 * LOG2E
B_SCALE = B_DIM ** -0.5 * LOG2E
C_SCALE = (C_NOPE + C_ROPE) ** -0.5 * LOG2E
M_SCALE = M_DIM ** -0.5

LANES = 128
C_HEAD_PAD = 128
ATT_TILE = 256
FF_CHUNK = 256
HALO = 16
VMEM_LIMIT = 56 * 1024 * 1024

BF16 = jnp.bfloat16
F32 = jnp.float32


def _params(n_grid):
    return pltpu.CompilerParams(dimension_semantics=("arbitrary",) * n_grid, vmem_limit_bytes=VMEM_LIMIT)


def _dot(a, b):
    return jnp.dot(a, b, preferred_element_type=F32)


def _dot_nt(a, b):
    return lax.dot_general(a, b, (((1,), (1,)), ((), ())), preferred_element_type=F32)


def _rms(x, g):
    ms = jnp.mean(x * x, axis=-1, keepdims=True)
    return x * lax.rsqrt(ms + EPS) * g


def _full(shape):
    n = len(shape)
    return pl.BlockSpec(shape, lambda *_: (0,) * n)


def _resident(shape):
    n = len(shape)
    return pl.BlockSpec(shape, lambda *_: (0,) * n, pipeline_mode=pl.Buffered(1))


def _row_tile(rows, cap):
    t = min(rows, cap)
    assert rows % t == 0
    return t


def _mla_expand(lat, kpe, wkn_ref, ek_ref, wcv_ref):
    lat16 = lat.astype(BF16)
    kcat = _dot(lat16, wkn_ref[...]) + _dot(kpe.astype(BF16), ek_ref[...])
    cv = _dot(lat16, wcv_ref[...])
    return kcat.astype(BF16), cv.astype(BF16)


def _inproj_kernel(x_ref, g_ref, cos_ref, sin_ref, wa_ref, wb_ref, wc_ref, gq_ref, wuq_ref, eq_ref,
                   gkv_ref, wkn_ref, ek_ref, wcv_ref,
                   aq_o, ak32_o, ak16_o, av32_o, av16_o, bq_o, bk32_o, bk16_o, bv32_o, bv16_o,
                   cq_o, lat_o, kpe_o, kcat_o, cv_o):
    h = _rms(x_ref[0], g_ref[...]).astype(BF16)
    na = A_HEADS * 2 * A_HALF
    aq_o[0] = (_dot(h, wa_ref[:, 0:na]) * A_SCALE).astype(BF16)
    ak = _dot(h, wa_ref[:, na:2 * na])
    ak32_o[0] = ak
    ak16_o[0] = ak.astype(BF16)
    av = _dot(h, wa_ref[:, 2 * na:3 * na])
    av32_o[0] = av
    av16_o[0] = av.astype(BF16)
    nb = B_HEADS * B_DIM
    bq_o[0] = (_dot(h, wb_ref[:, 0:nb]) * B_SCALE).astype(BF16)
    bk = _dot(h, wb_ref[:, nb:2 * nb])
    bk32_o[0] = bk
    bk16_o[0] = bk.astype(BF16)
    bv = _dot(h, wb_ref[:, 2 * nb:3 * nb])
    bv32_o[0] = bv
    bv16_o[0] = bv.astype(BF16)
    zc = _dot(h, wc_ref[...])
    cos4 = cos_ref[...]
    sin4 = sin_ref[...]
    hq = _rms(zc[:, 0:C_Q_LORA], gq_ref[...]).astype(BF16)
    nq = C_HEADS * C_HEAD_PAD
    zq = _dot(hq, wuq_ref[...])
    qpe = (zq[:, nq:nq + LANES] * cos4 + zq[:, nq + LANES:nq + 2 * LANES] * sin4) * C_SCALE
    cq_o[0] = (zq[:, 0:nq] * C_SCALE + _dot(qpe.astype(BF16), eq_ref[...])).astype(BF16)
    lat = _rms(zc[:, C_Q_LORA:C_Q_LORA + C_KV_LORA], gkv_ref[...])
    lat_o[0] = lat
    o = C_Q_LORA + C_KV_LORA
    kpe = zc[:, o:o + C_ROPE] * cos4[:, 0:C_ROPE] + zc[:, o + C_ROPE:o + 2 * C_ROPE] * sin4[:, 0:C_ROPE]
    kpe_o[0] = kpe
    kcat, cv = _mla_expand(lat, kpe, wkn_ref, ek_ref, wcv_ref)
    kcat_o[0] = kcat
    cv_o[0] = cv


def _inproj(x, lw, cos4, sin4):
    B, S, _ = x.shape
    tm = _row_tile(S, 512)
    grid = (B, S // tm)
    row = lambda w: pl.BlockSpec((1, tm, w), lambda b, i: (b, i, 0))
    pos = pl.BlockSpec((tm, LANES), lambda b, i: (i, 0))
    weights = [lw['g_mix_pre'], cos4, sin4, lw['wa'], lw['wb'], lw['wc'], lw['g_q'], lw['wuq'], lw['eq'],
               lw['g_kv'], lw['wkn'], lw['ek'], lw['wcv']]
    in_specs = [row(D_MODEL)] + [pos if w is cos4 or w is sin4 else _full(w.shape) for w in weights]
    widths = [(512, BF16), (512, F32), (512, BF16), (512, F32), (512, BF16),
              (256, BF16), (256, F32), (256, BF16), (256, F32), (256, BF16),
              (C_HEADS * C_HEAD_PAD, BF16), (C_KV_LORA, F32), (C_ROPE, F32),
              (C_HEADS * C_HEAD_PAD, BF16), (C_HEADS * C_VDIM, BF16)]
    return pl.pallas_call(
        _inproj_kernel, grid=grid, in_specs=in_specs,
        out_specs=[row(w) for w, _ in widths],
        out_shape=[jax.ShapeDtypeStruct((B, S, w), d) for w, d in widths],
        compiler_params=_params(2), name="inproj")(x, *weights)


def _expand_kernel(lat_ref, kpe_ref, wkn_ref, ek_ref, wcv_ref, kcat_o, cv_o):
    kcat, cv = _mla_expand(lat_ref[0], kpe_ref[0], wkn_ref, ek_ref, wcv_ref)
    kcat_o[0] = kcat
    cv_o[0] = cv


def _expand_cache(lat, kpe, lw):
    B, P, _ = lat.shape
    tm = _row_tile(P, 512)
    row = lambda w: pl.BlockSpec((1, tm, w), lambda b, i: (b, i, 0))
    ws = [lw['wkn'], lw['ek'], lw['wcv']]
    return pl.pallas_call(
        _expand_kernel, grid=(B, P // tm),
        in_specs=[row(C_KV_LORA), row(C_ROPE)] + [_full(w.shape) for w in ws],
        out_specs=[row(C_HEADS * C_HEAD_PAD), row(C_HEADS * C_VDIM)],
        out_shape=[jax.ShapeDtypeStruct((B, P, C_HEADS * C_HEAD_PAD), BF16),
                   jax.ShapeDtypeStruct((B, P, C_HEADS * C_VDIM), BF16)],
        compiler_params=_params(2), name="mla_expand")(lat, kpe, *ws)


def _lane_iota():
    return lax.broadcasted_iota(jnp.int32, (1, LANES), 1)


def _masked_streams(q, width):
    lane = _lane_iota()
    return [jnp.where(lane // width == s, q, jnp.zeros_like(q)) for s in range(LANES // width)]


def _with_ones(vt):
    first = _lane_iota() < 64
    one = jnp.ones_like(vt)
    return [jnp.where(first, vt, one), jnp.where(first, one, vt)]


def _normalise(acc):
    return acc / pltpu.roll(acc, 64, axis=1)


def _online_update(scs, vts, ms, accs):
    reps = scs[0].shape[1] // LANES
    ps, alphas, m_out = [], [], []
    for sc, m_prev in zip(scs, ms):
        m_next = jnp.maximum(m_prev, jnp.max(sc, axis=1, keepdims=True))
        alphas.append(jnp.exp2(m_prev - m_next))
        ps.append(jnp.exp2(sc - jnp.concatenate([m_next] * reps, axis=1)).astype(BF16))
        m_out.append(m_next)
    return m_out, [alpha * acc + _dot(p, vt1) for p, alpha, acc, vt1 in zip(ps, alphas, accs, vts)]


def _run_tiles(tiles, rows, scores, values, finish):
    scs = scores(*tiles[0])
    ms = accs = None
    for n, (i, j) in enumerate(tiles):
        if n == 0 or tiles[n - 1][0] != i:
            ms = [jnp.full((rows, LANES), NEG, F32)] * len(scs)
            accs = [jnp.zeros((rows, LANES), F32)] * len(scs)
        cur = scs
        last = n + 1 == len(tiles)
        if not last:
            scs = scores(*tiles[n + 1])
        ms, accs = _online_update(cur, values(j), ms, accs)
        if last or tiles[n + 1][0] != i:
            finish(i, [_normalise(acc) for acc in accs])


def _apply_bias(sc, bt):
    return jnp.where(bt > 0.5 * NEG, sc + bt, NEG)


def _softmax_pv(sc, vt1):
    m = jnp.max(sc, axis=1, keepdims=True)
    p = jnp.exp2(sc - m).astype(BF16)
    return _normalise(_dot(p, vt1))


def _diff_lambda(lam_ref):
    lf = lam_ref[...]
    lam_init = lf[4:5, 0:1]
    lam = (jnp.exp(jnp.sum(lf[0:1] * lf[1:2], axis=1, keepdims=True))
           - jnp.exp(jnp.sum(lf[2:3] * lf[3:4], axis=1, keepdims=True)) + lam_init)
    return lam, lam_init


def _finish_diff(outs, lam, lam_init, subln_ref):
    lane = _lane_iota()
    first = lane < A_VDIM
    o = jnp.where(first, outs[0] - lam * outs[1], outs[2] - lam * outs[3])
    sq = o * o
    ss0 = jnp.sum(jnp.where(first, sq, 0.0), axis=1, keepdims=True)
    ss1 = jnp.sum(jnp.where(first, 0.0, sq), axis=1, keepdims=True)
    ms = jnp.where(first, ss0, ss1) * (1.0 / A_VDIM)
    return (o * lax.rsqrt(ms + EPS) * subln_ref[...]) * (1.0 - lam_init)


def _tile(ref, n, t):
    return ref[0, n * t:(n + 1) * t, :]


def _attn_a_kernel(lam_ref, subln_ref, q_ref, k_ref, v_ref, bias_ref, o_ref, *, t, nq):
    lam, lam_init = _diff_lambda(lam_ref)
    qs = {}

    def scores(i, j):
        if i not in qs:
            qs.clear()
            qs[i] = _masked_streams(_tile(q_ref, i, t), A_HALF)
        kt = _tile(k_ref, j, t)
        scs = [_dot_nt(q, kt) for q in qs[i]]
        if j == i - 1:
            scs = [sc + bias_ref[s // 2, 1] for s, sc in enumerate(scs)]
        elif j == i:
            scs = [_apply_bias(sc, bias_ref[s // 2, 0]) for s, sc in enumerate(scs)]
        return scs

    def values(j):
        vt1 = _with_ones(_tile(v_ref, j, t))
        return [vt1[s // 2] for s in range(4)]

    def finish(i, outs):
        o_ref[0, i * t:(i + 1) * t, :] = _finish_diff(outs, lam, lam_init, subln_ref).astype(BF16)

    _run_tiles([(i, j) for i in range(nq) for j in range(i + 1)], t, scores, values, finish)


def _attn_a(aq, ak, av, bias, lam5, subln2):
    B, S, _ = aq.shape
    t = ATT_TILE
    assert S % t == 0
    blk = pl.BlockSpec((1, S, LANES), lambda b, p: (b, 0, p))
    return pl.pallas_call(
        functools.partial(_attn_a_kernel, t=t, nq=S // t), grid=(B, A_HEADS // 2),
        in_specs=[_full(lam5.shape), _full(subln2.shape), blk, blk, blk,
                  pl.BlockSpec((2, 2, t, t), lambda b, p: (p, 0, 0, 0))],
        out_specs=blk, out_shape=jax.ShapeDtypeStruct((B, S, A_HEADS * A_VDIM), BF16),
        compiler_params=_params(2), name="attn_diff")(lam5, subln2, aq, ak, av, bias)


def _first_lanes(outs):
    return jnp.where(_lane_iota() < 64, outs[0], outs[1]).astype(BF16)


def _attn_b_kernel(q_ref, k_ref, v_ref, bias_ref, o_ref, *, t, nq, n_back):
    qs = {}

    def scores(i, j):
        if i not in qs:
            qs.clear()
            qs[i] = _masked_streams(_tile(q_ref, i, t), B_DIM)
        kt = _tile(k_ref, j, t)
        return [_apply_bias(_dot_nt(q, kt), bias_ref[s, i - j]) for s, q in enumerate(qs[i])]

    def values(j):
        return _with_ones(_tile(v_ref, j, t))

    def finish(i, outs):
        o_ref[0, i * t:(i + 1) * t, :] = _first_lanes(outs)

    _run_tiles([(i, j) for i in range(nq) for j in range(max(i - n_back, 0), i + 1)], t, scores, values, finish)


def _attn_b(bq, bk, bv, bias):
    B, S, _ = bq.shape
    t = ATT_TILE
    n_back = bias.shape[1] - 1
    assert S % t == 0
    blk = pl.BlockSpec((1, S, LANES), lambda b, p: (b, 0, p))
    return pl.pallas_call(
        functools.partial(_attn_b_kernel, t=t, nq=S // t, n_back=n_back), grid=(B, B_HEADS // 2),
        in_specs=[blk, blk, blk, pl.BlockSpec((2, n_back + 1, t, t), lambda b, p: (p, 0, 0, 0))],
        out_specs=blk, out_shape=jax.ShapeDtypeStruct((B, S, B_HEADS * B_DIM), BF16),
        compiler_params=_params(2), name="attn_band")(bq, bk, bv, bias)


def _attn_c_kernel(q_ref, k_ref, v_ref, mask_ref, o_ref, *, t, nq):
    def scores(i, j):
        q = _tile(q_ref, i, t)
        kt = _tile(k_ref, j, t)
        scs = [_dot_nt(q[:, s * C_HEAD_PAD:(s + 1) * C_HEAD_PAD], kt[:, s * C_HEAD_PAD:(s + 1) * C_HEAD_PAD])
               for s in range(2)]
        return [_apply_bias(sc, mask_ref[...]) for sc in scs] if j == i else scs

    def values(j):
        return _with_ones(_tile(v_ref, j, t))

    def finish(i, outs):
        o_ref[0, i * t:(i + 1) * t, :] = _first_lanes(outs)

    _run_tiles([(i, j) for i in range(nq) for j in range(i + 1)], t, scores, values, finish)


def _attn_c(cq, kcat, cv, mask):
    B, S, _ = cq.shape
    t = ATT_TILE
    assert S % t == 0
    wide = pl.BlockSpec((1, S, 2 * C_HEAD_PAD), lambda b, p: (b, 0, p))
    blk = pl.BlockSpec((1, S, LANES), lambda b, p: (b, 0, p))
    return pl.pallas_call(
        functools.partial(_attn_c_kernel, t=t, nq=S // t), grid=(B, C_HEADS // 2),
        in_specs=[wide, wide, blk, _full(mask.shape)],
        out_specs=blk, out_shape=jax.ShapeDtypeStruct((B, S, C_HEADS * C_VDIM), BF16),
        compiler_params=_params(2), name="attn_mla")(cq, kcat, cv, mask)


def _dense_attn_kernel(*refs, mode):
    if mode == 'a':
        lam_ref, subln_ref, q_ref, k_ref, v_ref, bias_ref, o_ref = refs
    else:
        q_ref, k_ref, v_ref, bias_ref, o_ref = refs
    q = q_ref[0]
    k = k_ref[0]
    vt1 = _with_ones(v_ref[0])
    if mode == 'a':
        streams = [(qm, k, s // 2) for s, qm in enumerate(_masked_streams(q, A_HALF))]
    elif mode == 'b':
        streams = [(qm, k, s) for s, qm in enumerate(_masked_streams(q, B_DIM))]
    else:
        streams = [(q[:, s * C_HEAD_PAD:(s + 1) * C_HEAD_PAD], k[:, s * C_HEAD_PAD:(s + 1) * C_HEAD_PAD], s)
                   for s in range(2)]
    outs = [_softmax_pv(_apply_bias(_dot_nt(qm, km), bias_ref[hh]), vt1[hh]) for qm, km, hh in streams]
    if mode == 'a':
        lam, lam_init = _diff_lambda(lam_ref)
        o_ref[0] = _finish_diff(outs, lam, lam_init, subln_ref).astype(BF16)
    else:
        o_ref[0] = _first_lanes(outs)


def _dense_attn(mode, q, k, v, bias, n_heads, extra=()):
    B, T, _ = q.shape
    Kp = k.shape[1]
    qw = 2 * C_HEAD_PAD if mode == 'c' else LANES
    kernel = functools.partial(_dense_attn_kernel, mode=mode)
    return pl.pallas_call(
        kernel, grid=(B, n_heads // 2),
        in_specs=[_full(e.shape) for e in extra] + [
            pl.BlockSpec((1, T, qw), lambda b, p: (b, 0, p)),
            pl.BlockSpec((1, Kp, qw), lambda b, p: (b, 0, p)),
            pl.BlockSpec((1, Kp, LANES), lambda b, p: (b, 0, p)),
            pl.BlockSpec((2, T, Kp), lambda b, p: (p, 0, 0))],
        out_specs=pl.BlockSpec((1, T, LANES), lambda b, p: (b, 0, p)),
        out_shape=jax.ShapeDtypeStruct((B, T, n_heads * 64), BF16),
        compiler_params=_params(2), name="attn_dense_" + mode)(*extra, q, k, v, bias)


def _memkv_kernel(mem_ref, g_ref, wk_ref, wv_ref, k32_o, k16_o, v32_o, v16_o):
    m = _rms(mem_ref[0], g_ref[...]).astype(BF16)
    mk = _dot(m, wk_ref[...])
    mv = _dot(m, wv_ref[...])
    k32_o[0] = mk
    k16_o[0] = mk.astype(BF16)
    v32_o[0] = mv
    v16_o[0] = mv.astype(BF16)


def _memkv(mem, g, wk, wv):
    B = mem.shape[0]
    n = M_HEADS * M_DIM
    blk = pl.BlockSpec((1, MEM_LEN, n), lambda b: (b, 0, 0))
    return pl.pallas_call(
        _memkv_kernel, grid=(B,),
        in_specs=[pl.BlockSpec((1, MEM_LEN, D_MODEL), lambda b: (b, 0, 0)), _full(g.shape), _full(wk.shape),
                  _full(wv.shape)],
        out_specs=[blk] * 4,
        out_shape=[jax.ShapeDtypeStruct((B, MEM_LEN, n), d) for d in (F32, BF16, F32, BF16)],
        compiler_params=_params(1), name="mem_kv")(mem, g, wk, wv)


def _tail_kernel(x_ref, xh_ref, a_ref, ah_ref, b_ref, bh_ref, c_ref, ch_ref, mk_ref, mv_ref, past_ref,
                 wout_ref, gmix_ref, gxpre_ref, wq_ref, wo_ref, gxpost_ref,
                 gfpre_ref, wg_ref, wu_ref, cw_ref, cb_ref, wd_ref, gfpost_ref, o_ref, st_ref, *, tm):
    rows = HALO + tm
    first = pl.program_id(1) == 0
    cat = lambda halo_ref, ref: jnp.concatenate([halo_ref[0], ref[0]], axis=0)
    x = cat(xh_ref, x_ref)
    na = A_HEADS * A_VDIM
    nb = na + B_HEADS * B_DIM
    y = (_dot(cat(ah_ref, a_ref), wout_ref[0:na, :]) + _dot(cat(bh_ref, b_ref), wout_ref[na:nb, :])
         + _dot(cat(ch_ref, c_ref), wout_ref[nb:D_MODEL, :]))
    x = x + _rms(y, gmix_ref[...])
    h = _rms(x, gxpre_ref[...]).astype(BF16)
    q = (_dot(h, wq_ref[...]) * (M_SCALE * LOG2E)).astype(BF16)
    pairs = []
    for p in range(M_HEADS // 2):
        sl = slice(p * LANES, (p + 1) * LANES)
        kp = mk_ref[0, :, sl]
        vp1 = _with_ones(mv_ref[0, :, sl])
        outs = [_softmax_pv(_dot_nt(qm, kp), vp1[s]) for s, qm in enumerate(_masked_streams(q[:, sl], M_DIM))]
        pairs.append(_first_lanes(outs))
    x = x + _rms(_dot(jnp.concatenate(pairs, axis=1), wo_ref[...]), gxpost_ref[...])
    h = _rms(x, gfpre_ref[...]).astype(BF16)
    row = lax.broadcasted_iota(jnp.int32, (rows, 1), 0)
    past_row = [jnp.logical_and(first, row == HALO - 2), jnp.logical_and(first, row == HALO - 1)]
    n_chunks = D_FF // FF_CHUNK

    def up(c):
        sl = slice(c * FF_CHUNK, (c + 1) * FF_CHUNK)
        return _dot(h, wg_ref[:, sl]), _dot(h, wu_ref[:, sl])

    acc = jnp.zeros((rows, D_MODEL), F32)
    nxt = up(0)
    for c in range(n_chunks):
        sl = slice(c * FF_CHUNK, (c + 1) * FF_CHUNK)
        g, u = nxt
        if c + 1 < n_chunks:
            nxt = up(c + 1)
        past = past_ref[0, :, sl]
        g = jnp.where(past_row[0], past[0:1], jnp.where(past_row[1], past[1:2], g))
        cw = cw_ref[:, sl]
        cc = (cb_ref[:, sl] + pltpu.roll(g, 2, axis=0) * cw[0:1] + pltpu.roll(g, 1, axis=0) * cw[1:2]
              + g * cw[2:3])
        acc = acc + _dot((cc * jax.nn.sigmoid(cc) * u).astype(BF16), wd_ref[sl, :])
        st_ref[0, :, sl] = g[rows - 2:rows]
    o_ref[0] = (x + _rms(acc, gfpost_ref[...]))[HALO:]


def _tail(x, a, b, c, mk, mv, past, lw):
    B, S, _ = x.shape
    tm = _row_tile(S, 512)
    assert tm % HALO == 0
    hb = tm // HALO
    row = lambda w: pl.BlockSpec((1, tm, w), lambda bb, i: (bb, i, 0))
    halo = lambda w: pl.BlockSpec((1, HALO, w), lambda bb, i: (bb, jnp.maximum(i * hb - 1, 0), 0))
    mem = pl.BlockSpec((1, MEM_LEN, M_HEADS * M_DIM), lambda bb, i: (bb, 0, 0))
    st = pl.BlockSpec((1, CONV_W - 1, D_FF), lambda bb, i: (bb, 0, 0))
    ws = [lw[k] for k in ('w_out', 'g_mix_post', 'g_x_pre', 'w_xq', 'w_xo', 'g_x_post',
                          'g_f_pre', 'w_gate', 'w_up', 'conv_w', 'conv_b', 'w_down', 'g_f_post')]
    acts = []
    specs = []
    for arr in (x, a, b, c):
        acts += [arr, arr]
        specs += [row(arr.shape[2]), halo(arr.shape[2])]
    return pl.pallas_call(
        functools.partial(_tail_kernel, tm=tm), grid=(B, S // tm),
        in_specs=specs + [mem, mem, st] + [_resident(w.shape) for w in ws],
        out_specs=[row(D_MODEL), st],
        out_shape=[jax.ShapeDtypeStruct(x.shape, F32), jax.ShapeDtypeStruct((B, CONV_W - 1, D_FF), F32)],
        compiler_params=_params(2), name="tail")(*acts, mk, mv, past, *ws)


def _rot_cols(w):
    half = C_ROPE // 2
    return jnp.concatenate([-w[..., half:], w[..., :half]], axis=-1)


def _layer_weights(l, w_in, w_out, norms, mla_q_norm, mla_w_uq, mla_kv_norm, mla_w_ukv, w_xq, w_mk, w_mv, w_xo,
                   w_gate, w_up, conv_w, conv_b, w_down):
    wi = w_in[l]
    na, nb = 3 * A_HEADS * 2 * A_HALF, 3 * B_HEADS * B_DIM
    wcq = wi[:, na + nb:na + nb + C_Q_LORA]
    wckv = wi[:, na + nb + C_Q_LORA:na + nb + C_Q_LORA + C_KV_LORA]
    wcpe = wi[:, na + nb + C_Q_LORA + C_KV_LORA:]
    wc = jnp.concatenate([wcq, wckv, wcpe, _rot_cols(wcpe),
                          jnp.zeros((D_MODEL, 512 - C_Q_LORA - C_KV_LORA - 2 * C_ROPE), F32)], axis=1)
    uq = mla_w_uq[l].reshape(C_Q_LORA, C_HEADS, C_NOPE + C_ROPE)
    uq_n = jnp.pad(uq[..., :C_NOPE], ((0, 0), (0, 0), (0, C_HEAD_PAD - C_NOPE))).reshape(C_Q_LORA, -1)
    uq_p = uq[..., C_NOPE:]
    wuq = jnp.concatenate([uq_n, uq_p.reshape(C_Q_LORA, -1), _rot_cols(uq_p).reshape(C_Q_LORA, -1)], axis=1)
    eye = jnp.eye(C_ROPE, dtype=F32)
    place = jnp.pad(eye, ((0, 0), (C_NOPE, C_HEAD_PAD - C_NOPE - C_ROPE)))
    eq = jnp.kron(jnp.eye(C_HEADS, dtype=F32), place)
    ek = jnp.tile(place, (1, C_HEADS))
    ukv = mla_w_ukv[l].reshape(C_KV_LORA, C_HEADS, C_NOPE + C_VDIM)
    wkn = jnp.pad(ukv[..., :C_NOPE], ((0, 0), (0, 0), (0, C_HEAD_PAD - C_NOPE))).reshape(C_KV_LORA, -1)
    wcv = ukv[..., C_NOPE:].reshape(C_KV_LORA, -1)
    g = norms[l]
    row = lambda v: v.reshape(1, -1).astype(F32)
    bf = lambda v: v.astype(BF16)
    return dict(
        wa=bf(wi[:, :na]), wb=bf(wi[:, na:na + nb]), wc=bf(wc), wuq=bf(wuq), eq=bf(eq), ek=bf(ek),
        wkn=bf(wkn), wcv=bf(wcv), g_q=row(mla_q_norm[l]), g_kv=row(mla_kv_norm[l]),
        g_mix_pre=row(g[N_MIX_PRE]), g_mix_post=row(g[N_MIX_POST]), g_x_pre=row(g[N_X_PRE]),
        g_x_post=row(g[N_X_POST]), g_f_pre=row(g[N_F_PRE]), g_f_post=row(g[N_F_POST]), g_mem=row(g[N_MEM]),
        w_out=bf(w_out[l]), w_xq=bf(w_xq[l]), w_mk=bf(w_mk[l]), w_mv=bf(w_mv[l]), w_xo=bf(w_xo[l]),
        w_gate=bf(w_gate[l]), w_up=bf(w_up[l]), w_down=bf(w_down[l]),
        conv_w=conv_w[l].astype(F32), conv_b=row(conv_b[l]))


def _rope_tables(pos):
    half = C_ROPE // 2
    inv = ROPE_THETA ** (-jnp.arange(half, dtype=F32) / half)
    ang = pos.astype(F32)[:, None] * inv[None, :]
    reps = LANES // half
    return jnp.tile(jnp.cos(ang), (1, reps)), jnp.tile(jnp.sin(ang), (1, reps))


def _t5_bucket(rel):
    half = T5_BUCKETS // 2
    exact = half // 2
    n = jnp.abs(rel)
    n2 = jnp.maximum(n * n, 1)
    log2_n2 = 31 - lax.clz(n2.astype(jnp.int32))
    large = jnp.minimum(log2_n2 + 2, half - 1)
    return jnp.where(rel > 0, half, 0) + jnp.where(n < exact, n, large)


def _chunk_causal(qpos, kpos):
    return (kpos[None, :] // CHUNK) <= (qpos[:, None] // CHUNK)


def _t5_lookup(t5_table):
    return lambda rel: t5_table[_t5_bucket(rel)].astype(F32).T


def _band_lookup(rel_table):
    return lambda rel: rel_table[:, jnp.clip(rel, -B_REL_CLIP, B_REL_CLIP) + B_REL_CLIP].astype(F32)


def _toeplitz_tile(lookup, t, d):
    m = jnp.arange(2 * t, dtype=jnp.int32)
    v = lookup(d + m - 2 * t * (m >= t))
    h = v.shape[0]
    strip = jnp.broadcast_to(v[:, None, :], (h, t, 2 * t)).reshape(h, 2 * t * t)
    return strip[:, :t * (2 * t - 1)].reshape(h, t, 2 * t - 1)[:, :, :t]


def _toeplitz_rows(lookup, q0, nq, k0, nk):
    w = lookup(k0 - q0 - (nq - 1) + jnp.arange(nk + nq - 1, dtype=jnp.int32))
    return jnp.stack([w[:, nq - 1 - i:nq - 1 - i + nk] for i in range(nq)], axis=1)


def _band_mask(qpos, kpos):
    qc = qpos[:, None] // CHUNK
    kc = kpos[None, :] // CHUNK
    return (kpos[None, :] >= 0) & (kc <= qc) & (kc >= qc - B_LEFT_CHUNKS)


def _masked(bias, mask):
    return jnp.where(mask, bias * LOG2E, NEG)


def _layer_prompt(x, mem, lw, lam, subln2, t5_table, rel_table):
    B, S, _ = x.shape
    t = ATT_TILE
    pos = jnp.arange(S, dtype=jnp.int32)
    cos4, sin4 = _rope_tables(pos)
    (aq, ak32, ak16, av32, av16, bq, bk32, bk16, bv32, bv16, cq, lat, kpe, kcat, cv) = _inproj(x, lw, cos4, sin4)
    tpos = jnp.arange(t, dtype=jnp.int32)
    causal = _chunk_causal(tpos, tpos)
    assert t + 1 >= T5_MAX_DIST
    t5 = _t5_lookup(t5_table)
    far = t5(jnp.full((1,), -(t + 1), jnp.int32))[:, :, None]
    bias_a = jnp.stack([_masked(_toeplitz_tile(t5, t, 0) - far, causal[None]),
                        (_toeplitz_tile(t5, t, -t) - far) * LOG2E], axis=1)
    a = _attn_a(aq, ak16, av16, bias_a, lam, subln2)
    n_back = -(-B_LEFT_CHUNKS * CHUNK // t)
    band = _band_lookup(rel_table)
    bias_b = jnp.stack([_masked(_toeplitz_tile(band, t, -k * t),
                                _band_mask(tpos + n_back * t, tpos + (n_back - k) * t)[None])
                        for k in range(n_back + 1)], axis=1)
    b = _attn_b(bq, bk16, bv16, bias_b)
    c = _attn_c(cq, kcat, cv, _masked(jnp.zeros((t, t), F32), causal))
    mk32, mk16, mv32, mv16 = _memkv(mem, lw['g_mem'], lw['w_mk'], lw['w_mv'])
    x, conv_state = _tail(x, a, b, c, mk16, mv16, jnp.zeros((B, CONV_W - 1, D_FF), F32), lw)
    nb = min(B_LEFT_CHUNKS * CHUNK, S)
    state = (ak32.reshape(B, S, A_HEADS, 2 * A_HALF), av32.reshape(B, S, A_HEADS, A_VDIM),
             bk32[:, S - nb:].reshape(B, nb, B_HEADS, B_DIM), bv32[:, S - nb:].reshape(B, nb, B_HEADS, B_DIM),
             lat, kpe, mk32.reshape(B, MEM_LEN, M_HEADS, M_DIM), mv32.reshape(B, MEM_LEN, M_HEADS, M_DIM),
             conv_state)
    return x, state


def _pad_keys(cache, new, kp):
    full = jnp.concatenate([cache.astype(BF16), new], axis=1)
    return jnp.pad(full, ((0, 0), (0, kp - full.shape[1]), (0, 0)))


def _layer_sample(x, cache, lw, lam, subln2, t5_table, rel_table):
    ak_c, av_c, bk_c, bv_c, lat_c, kpe_c, mk_c, mv_c, conv_c = cache
    B, T, _ = x.shape
    P = ak_c.shape[1]
    nb = bk_c.shape[1]
    pos = P + jnp.arange(T, dtype=jnp.int32)
    cos4, sin4 = _rope_tables(pos)
    (aq, ak32, ak16, av32, av16, bq, bk32, bk16, bv32, bv16, cq, lat, kpe, kcat, cv) = _inproj(x, lw, cos4, sin4)
    kp = -(-(P + T) // ATT_TILE) * ATT_TILE
    kpos = jnp.arange(kp, dtype=jnp.int32)
    valid = kpos < P + T
    causal = _chunk_causal(pos, kpos) & valid[None, :]
    bias_a = _masked(_toeplitz_rows(_t5_lookup(t5_table), P, T, 0, kp), causal[None])
    a = _dense_attn('a', aq, _pad_keys(ak_c.reshape(B, P, -1), ak16, kp), _pad_keys(av_c.reshape(B, P, -1), av16, kp),
                    bias_a, A_HEADS, extra=(lam, subln2))
    kpb = -(-(nb + T) // LANES) * LANES
    bkpos = (P - nb) + jnp.arange(kpb, dtype=jnp.int32)
    bias_b = _masked(_toeplitz_rows(_band_lookup(rel_table), P, T, P - nb, kpb),
                     (_band_mask(pos, bkpos) & (bkpos < P + T)[None, :])[None])
    b = _dense_attn('b', bq, _pad_keys(bk_c.reshape(B, nb, -1), bk16, kpb), _pad_keys(bv_c.reshape(B, nb, -1), bv16, kpb),
                    bias_b, B_HEADS)
    kcat_c, cv_c = _expand_cache(lat_c, kpe_c, lw)
    mask_c = jnp.broadcast_to(_masked(jnp.zeros((T, kp), F32), causal)[None], (C_HEADS, T, kp))
    c = _dense_attn('c', cq, _pad_keys(kcat_c, kcat, kp), _pad_keys(cv_c, cv, kp), mask_c, C_HEADS)
    x, conv_state = _tail(x, a, b, c, mk_c.reshape(B, MEM_LEN, -1).astype(BF16), mv_c.reshape(B, MEM_LEN, -1).astype(BF16),
                          conv_c.astype(F32), lw)
    state = (ak32.reshape(B, T, A_HEADS, 2 * A_HALF), av32.reshape(B, T, A_HEADS, A_VDIM),
             bk32.reshape(B, T, B_HEADS, B_DIM), bv32.reshape(B, T, B_HEADS, B_DIM), lat, kpe, conv_state)
    return x, state


def kernel(x_prompt, x_sample, cache_a_k, cache_a_v, cache_b_k, cache_b_v, cache_c_latent, cache_c_rope_k, cache_mem_k, cache_mem_v, state_ffn_conv, mem_prompt, w_in, w_out, norms, diff_lambda, diff_subln, t5_bias, band_rel_bias, mla_q_norm, mla_w_uq, mla_kv_norm, mla_w_ukv, w_xq, w_mk, w_mv, w_xo, w_gate, w_up, conv_w, conv_b, w_down):
    xp, xs = x_prompt, x_sample
    sp_all, ss_all = [], []
    for l in range(DEPTH):
        lw = _layer_weights(l, w_in, w_out, norms, mla_q_norm, mla_w_uq, mla_kv_norm, mla_w_ukv, w_xq, w_mk, w_mv,
                            w_xo, w_gate, w_up, conv_w, conv_b, w_down)
        lam_init = 0.8 - 0.6 * math.exp(-0.3 * l)
        lam = jnp.concatenate([diff_lambda[l].astype(F32), jnp.full((1, A_HALF), lam_init, F32)], axis=0)
        subln2 = jnp.tile(diff_subln[l].astype(F32).reshape(1, -1), (1, LANES // A_VDIM))
        xp, sp = _layer_prompt(xp, mem_prompt, lw, lam, subln2, t5_bias, band_rel_bias[l])
        cache_l = (cache_a_k[l], cache_a_v[l], cache_b_k[l], cache_b_v[l], cache_c_latent[l],
                   cache_c_rope_k[l], cache_mem_k[l], cache_mem_v[l], state_ffn_conv[l])
        xs, ss = _layer_sample(xs, cache_l, lw, lam, subln2, t5_bias, band_rel_bias[l])
        sp_all.append(sp)
        ss_all.append(ss)

    def stk(states, i):
        return jnp.stack([s[i] for s in states])

    return (xp, xs) + tuple(stk(sp_all, i) for i in range(9)) + tuple(stk(ss_all, i) for i in range(7))
```

```python
import functools
import math

import jax
import jax.numpy as jnp
import numpy as np
from jax import lax
from jax.experimental import pallas as pl
from jax.experimental.pallas import tpu as pltpu

D_MODEL = 1024
DEPTH = 2
CHUNK = 64
A_HEADS, A_HALF, A_VDIM = 8, 32, 64
B_HEADS, B_DIM, B_LEFT_CHUNKS, B_REL_CLIP = 4, 64, 8, 128
C_HEADS, C_NOPE, C_ROPE, C_VDIM, C_Q_LORA, C_KV_LORA = 4, 64, 32, 64, 256, 128
ROPE_THETA = 10000.0
T5_BUCKETS = 32
T5_MAX_DIST = 128
MEM_LEN, M_HEADS, M_DIM = 256, 4, 64
D_FF = 2816
CONV_W = 3
EPS = 1e-6
NEG = -1e30
N_MIX_PRE, N_MIX_POST, N_X_PRE, N_X_POST, N_F_PRE, N_F_POST, N_MEM = 0, 1, 2, 3, 4, 5, 6

LOG2E = math.log2(math.e)
A_SCALE = A_HALF ** -0.5 * LOG2E
B_SCALE = B_DIM ** -0.5 * LOG2E
C_SCALE = (C_NOPE + C_ROPE) ** -0.5 * LOG2E
M_SCALE = M_DIM ** -0.5

LANES = 128
C_HEAD_PAD = 128
ATT_TILE = 256
FAR_SPAN = 4
FF_CHUNK = 512
HALO = 16
VMEM_LIMIT = 56 * 1024 * 1024

BF16 = jnp.bfloat16
F32 = jnp.float32


def _params(n_grid):
    return pltpu.CompilerParams(dimension_semantics=("arbitrary",) * n_grid, vmem_limit_bytes=VMEM_LIMIT)


def _dot(a, b):
    return jnp.dot(a, b, preferred_element_type=F32)


def _dot_nt(a, b):
    return lax.dot_general(a, b, (((1,), (1,)), ((), ())), preferred_element_type=F32)


def _rms(x, g):
    ms = jnp.mean(x * x, axis=-1, keepdims=True)
    return x * lax.rsqrt(ms + EPS) * g


def _full(shape):
    n = len(shape)
    return pl.BlockSpec(shape, lambda *_: (0,) * n)


def _resident(shape):
    n = len(shape)
    return pl.BlockSpec(shape, lambda *_: (0,) * n, pipeline_mode=pl.Buffered(1))


def _row_tile(rows, cap):
    t = min(rows, cap)
    assert rows % t == 0
    return t


def _mla_expand(lat, kpe, wkn_ref, ek_ref, wcv_ref):
    lat16 = lat.astype(BF16)
    kcat = _dot(lat16, wkn_ref[...]) + _dot(kpe.astype(BF16), ek_ref[...])
    return kcat.astype(BF16), _dot(lat16, wcv_ref[...])


def _inproj_kernel(x_ref, g_ref, cos_ref, sin_ref, wa_ref, wb_ref, wc_ref, gq_ref, wuq_ref, eq_ref,
                   gkv_ref, wkn_ref, ek_ref, wcv_ref,
                   aq_o, ak32_o, ak16_o, av32_o, av16_o, bq_o, bk32_o, bk16_o, bv32_o, bv16_o,
                   cq_o, lat_o, kpe_o, kcat_o, cv_o, *, values_t):
    val16 = (lambda v: v.astype(BF16).T) if values_t else (lambda v: v.astype(BF16))
    h = _rms(x_ref[0], g_ref[...]).astype(BF16)
    na = A_HEADS * 2 * A_HALF
    aq_o[0] = (_dot(h, wa_ref[:, 0:na]) * A_SCALE).astype(BF16)
    ak = _dot(h, wa_ref[:, na:2 * na])
    ak32_o[0] = ak
    ak16_o[0] = ak.astype(BF16)
    av = _dot(h, wa_ref[:, 2 * na:3 * na])
    av32_o[0] = av
    av16_o[0] = val16(av)
    nb = B_HEADS * B_DIM
    bq_o[0] = (_dot(h, wb_ref[:, 0:nb]) * B_SCALE).astype(BF16)
    bk = _dot(h, wb_ref[:, nb:2 * nb])
    bk32_o[0] = bk
    bk16_o[0] = bk.astype(BF16)
    bv = _dot(h, wb_ref[:, 2 * nb:3 * nb])
    bv32_o[0] = bv
    bv16_o[0] = val16(bv)
    zc = _dot(h, wc_ref[...])
    cos4 = cos_ref[...]
    sin4 = sin_ref[...]
    hq = _rms(zc[:, 0:C_Q_LORA], gq_ref[...]).astype(BF16)
    nq = C_HEADS * C_HEAD_PAD
    zq = _dot(hq, wuq_ref[...])
    qpe = (zq[:, nq:nq + LANES] * cos4 + zq[:, nq + LANES:nq + 2 * LANES] * sin4) * C_SCALE
    cq_o[0] = (zq[:, 0:nq] * C_SCALE + _dot(qpe.astype(BF16), eq_ref[...])).astype(BF16)
    lat = _rms(zc[:, C_Q_LORA:C_Q_LORA + C_KV_LORA], gkv_ref[...])
    lat_o[0] = lat
    o = C_Q_LORA + C_KV_LORA
    kpe = zc[:, o:o + C_ROPE] * cos4[:, 0:C_ROPE] + zc[:, o + C_ROPE:o + 2 * C_ROPE] * sin4[:, 0:C_ROPE]
    kpe_o[0] = kpe
    kcat, cv = _mla_expand(lat, kpe, wkn_ref, ek_ref, wcv_ref)
    kcat_o[0] = kcat
    cv_o[0] = val16(cv)


def _inproj(x, lw, cos4, sin4, values_t):
    B, S, _ = x.shape
    tm = _row_tile(S, 512)
    grid = (B, S // tm)
    row = lambda w: pl.BlockSpec((1, tm, w), lambda b, i: (b, i, 0))
    pos = pl.BlockSpec((tm, LANES), lambda b, i: (i, 0))
    weights = [lw['g_mix_pre'], cos4, sin4, lw['wa'], lw['wb'], lw['wc'], lw['g_q'], lw['wuq'], lw['eq'],
               lw['g_kv'], lw['wkn'], lw['ek'], lw['wcv']]
    in_specs = [row(D_MODEL)] + [pos if w is cos4 or w is sin4 else _resident(w.shape) for w in weights]
    widths = [(512, BF16), (512, F32), (512, BF16), (512, F32), (512, BF16),
              (256, BF16), (256, F32), (256, BF16), (256, F32), (256, BF16),
              (C_HEADS * C_HEAD_PAD, BF16), (C_KV_LORA, F32), (C_ROPE, F32),
              (C_HEADS * C_HEAD_PAD, BF16), (C_HEADS * C_VDIM, BF16)]
    out_specs = [row(w) for w, _ in widths]
    out_shape = [jax.ShapeDtypeStruct((B, S, w), d) for w, d in widths]
    if values_t:
        for n in (4, 9, 14):
            w = widths[n][0]
            out_specs[n] = pl.BlockSpec((1, w, tm), lambda b, i: (b, 0, i))
            out_shape[n] = jax.ShapeDtypeStruct((B, w, S), BF16)
    return pl.pallas_call(
        functools.partial(_inproj_kernel, values_t=values_t), grid=grid, in_specs=in_specs,
        out_specs=out_specs, out_shape=out_shape,
        compiler_params=_params(2), name="inproj")(x, *weights)


def _expand_kernel(lat_ref, kpe_ref, wkn_ref, ek_ref, wcv_ref, kcat_o, cv_o):
    kcat, cv = _mla_expand(lat_ref[0], kpe_ref[0], wkn_ref, ek_ref, wcv_ref)
    kcat_o[0] = kcat
    cv_o[0] = cv.astype(BF16)


def _expand_cache(lat, kpe, lw):
    B, P, _ = lat.shape
    tm = _row_tile(P, 512)
    row = lambda w: pl.BlockSpec((1, tm, w), lambda b, i: (b, i, 0))
    ws = [lw['wkn'], lw['ek'], lw['wcv']]
    return pl.pallas_call(
        _expand_kernel, grid=(B, P // tm),
        in_specs=[row(C_KV_LORA), row(C_ROPE)] + [_full(w.shape) for w in ws],
        out_specs=[row(C_HEADS * C_HEAD_PAD), row(C_HEADS * C_VDIM)],
        out_shape=[jax.ShapeDtypeStruct((B, P, C_HEADS * C_HEAD_PAD), BF16),
                   jax.ShapeDtypeStruct((B, P, C_HEADS * C_VDIM), BF16)],
        compiler_params=_params(2), name="mla_expand")(lat, kpe, *ws)


def _lane_iota():
    return lax.broadcasted_iota(jnp.int32, (1, LANES), 1)


def _masked_streams(q, width):
    lane = _lane_iota()
    return [jnp.where(lane // width == s, q, jnp.zeros_like(q)) for s in range(LANES // width)]


def _with_ones(vt):
    first = _lane_iota() < 64
    one = jnp.ones_like(vt)
    return [jnp.where(first, vt, one), jnp.where(first, one, vt)]


def _normalise(acc):
    return acc / pltpu.roll(acc, 64, axis=1)


def _with_ones_t(vt):
    first = lax.broadcasted_iota(jnp.int32, (LANES, 1), 0) < 64
    one = jnp.ones_like(vt)
    return [jnp.where(first, vt, one), jnp.where(first, one, vt)]


def _normalise_t(acc, head):
    return acc[0:64] / acc[64:128] if head == 0 else acc[64:128] / acc[0:64]


def _online_update_t(scs, vts, ms, accs):
    ps, alphas, m_out = [], [], []
    for sc, m_prev in zip(scs, ms):
        m_next = jnp.maximum(m_prev, jnp.max(sc, axis=0, keepdims=True)).astype(BF16)
        ps.append(jnp.exp2(sc.astype(BF16) - m_next))
        m_next = m_next.astype(F32)
        alphas.append(jnp.exp2(m_prev - m_next))
        m_out.append(m_next)
    return m_out, [alpha * acc + _dot(vt1, p) for p, alpha, acc, vt1 in zip(ps, alphas, accs, vts)]


def _run_tiles(tiles, rows, scores, values, finish):
    scs = scores(*tiles[0])
    ms = accs = None
    for n, (i, j, w) in enumerate(tiles):
        if n == 0 or tiles[n - 1][0] != i:
            ms = [jnp.full((1, rows), NEG, F32)] * len(scs)
            accs = [jnp.zeros((LANES, rows), F32)] * len(scs)
        cur = scs
        last = n + 1 == len(tiles)
        if not last:
            scs = scores(*tiles[n + 1])
        ms, accs = _online_update_t(cur, values(j, w), ms, accs)
        if last or tiles[n + 1][0] != i:
            finish(i, accs)


def _causal_spans(nq, n_biased):
    tiles = []
    for i in range(nq):
        n_far = max(i + 1 - n_biased, 0)
        tiles += [(i, j, min(FAR_SPAN, n_far - j)) for j in range(0, n_far, FAR_SPAN)]
        tiles += [(i, j, 1) for j in range(n_far, i + 1)]
    return tiles


def _apply_bias(sc, bt):
    return jnp.where(bt > 0.5 * NEG, sc + bt, NEG)


def _softmax_pv(sc, vt1):
    m = jnp.max(sc, axis=1, keepdims=True)
    p = jnp.exp2(sc - m).astype(BF16)
    return _normalise(_dot(p, vt1))


def _diff_lambda(lam_ref):
    lf = lam_ref[...]
    lam_init = lf[4:5, 0:1]
    lam = (jnp.exp(jnp.sum(lf[0:1] * lf[1:2], axis=1, keepdims=True))
           - jnp.exp(jnp.sum(lf[2:3] * lf[3:4], axis=1, keepdims=True)) + lam_init)
    return lam, lam_init


def _finish_diff(o, lam_init, subln_ref):
    first = _lane_iota() < A_VDIM
    sq = o * o
    ss0 = jnp.sum(jnp.where(first, sq, 0.0), axis=1, keepdims=True)
    ss1 = jnp.sum(jnp.where(first, 0.0, sq), axis=1, keepdims=True)
    ms = jnp.where(first, ss0, ss1) * (1.0 / A_VDIM)
    return (o * lax.rsqrt(ms + EPS) * subln_ref[...]) * (1.0 - lam_init)


def _tile(ref, n, t, w=1):
    return ref[0, n * t:(n + w) * t, :]


def _tile_t(ref, n, t, w=1):
    return ref[0, :, n * t:(n + w) * t]


def _pair_rows(o_h0, o_h1):
    return jnp.concatenate([o_h0, o_h1], axis=0).T


def _attn_a_kernel(lam_ref, subln_ref, q_ref, k_ref, vt_ref, bias_ref, o_ref, *, t, nq):
    lam, lam_init = _diff_lambda(lam_ref)
    qs = {}

    def scores(i, j, w):
        if i not in qs:
            qs.clear()
            qs[i] = _masked_streams(_tile(q_ref, i, t), A_HALF)
        kt = _tile(k_ref, j, t, w)
        scs = [_dot_nt(kt, q) for q in qs[i]]
        if j == i - 1:
            scs = [sc + bias_ref[s // 2, 1] for s, sc in enumerate(scs)]
        elif j == i:
            scs = [_apply_bias(sc, bias_ref[s // 2, 0]) for s, sc in enumerate(scs)]
        return scs

    def values(j, w):
        vt1 = _with_ones_t(_tile_t(vt_ref, j, t, w))
        return [vt1[s // 2] for s in range(4)]

    def finish(i, accs):
        n = [_normalise_t(acc, s // 2) for s, acc in enumerate(accs)]
        o = _pair_rows(n[0] - lam * n[1], n[2] - lam * n[3])
        o_ref[0, i * t:(i + 1) * t, :] = _finish_diff(o, lam_init, subln_ref).astype(BF16)

    _run_tiles(_causal_spans(nq, 2), t, scores, values, finish)


def _attn_a(aq, ak, avt, bias, lam5, subln2):
    B, S, _ = aq.shape
    t = ATT_TILE
    assert S % t == 0
    blk = pl.BlockSpec((1, S, LANES), lambda b, p: (b, 0, p))
    return pl.pallas_call(
        functools.partial(_attn_a_kernel, t=t, nq=S // t), grid=(B, A_HEADS // 2),
        in_specs=[_full(lam5.shape), _full(subln2.shape), blk, blk,
                  pl.BlockSpec((1, LANES, S), lambda b, p: (b, p, 0)),
                  pl.BlockSpec((2, 2, t, t), lambda b, p: (p, 0, 0, 0))],
        out_specs=blk, out_shape=jax.ShapeDtypeStruct((B, S, A_HEADS * A_VDIM), BF16),
        compiler_params=_params(2), name="attn_diff")(lam5, subln2, aq, ak, avt, bias)


def _first_lanes(outs):
    return jnp.where(_lane_iota() < 64, outs[0], outs[1]).astype(BF16)


def _attn_b_kernel(q_ref, k_ref, vt_ref, bias_ref, o_ref, *, t, nq, n_back):
    qs = {}

    def scores(i, j, w):
        if i not in qs:
            qs.clear()
            qs[i] = _masked_streams(_tile(q_ref, i, t), B_DIM)
        kt = _tile(k_ref, j, t)
        return [_apply_bias(_dot_nt(kt, q), bias_ref[s, i - j]) for s, q in enumerate(qs[i])]

    def values(j, w):
        return _with_ones_t(_tile_t(vt_ref, j, t))

    def finish(i, accs):
        o = _pair_rows(_normalise_t(accs[0], 0), _normalise_t(accs[1], 1))
        o_ref[0, i * t:(i + 1) * t, :] = o.astype(BF16)

    _run_tiles([(i, j, 1) for i in range(nq) for j in range(max(i - n_back, 0), i + 1)], t, scores, values, finish)


def _attn_b(bq, bk, bvt, bias):
    B, S, _ = bq.shape
    t = ATT_TILE
    n_back = bias.shape[1] - 1
    assert S % t == 0
    blk = pl.BlockSpec((1, S, LANES), lambda b, p: (b, 0, p))
    return pl.pallas_call(
        functools.partial(_attn_b_kernel, t=t, nq=S // t, n_back=n_back), grid=(B, B_HEADS // 2),
        in_specs=[blk, blk, pl.BlockSpec((1, LANES, S), lambda b, p: (b, p, 0)),
                  pl.BlockSpec((2, n_back + 1, t, t), lambda b, p: (p, 0, 0, 0))],
        out_specs=blk, out_shape=jax.ShapeDtypeStruct((B, S, B_HEADS * B_DIM), BF16),
        compiler_params=_params(2), name="attn_band")(bq, bk, bvt, bias)


def _attn_c_kernel(q_ref, k_ref, vt_ref, mask_ref, o_ref, *, t, nq):
    def scores(i, j, w):
        q = _tile(q_ref, i, t)
        kt = _tile(k_ref, j, t, w)
        scs = [_dot_nt(kt[:, s * C_HEAD_PAD:(s + 1) * C_HEAD_PAD], q[:, s * C_HEAD_PAD:(s + 1) * C_HEAD_PAD])
               for s in range(2)]
        return [_apply_bias(sc, mask_ref[...]) for sc in scs] if j == i else scs

    def values(j, w):
        return _with_ones_t(_tile_t(vt_ref, j, t, w))

    def finish(i, accs):
        o = _pair_rows(_normalise_t(accs[0], 0), _normalise_t(accs[1], 1))
        o_ref[0, i * t:(i + 1) * t, :] = o.astype(BF16)

    _run_tiles(_causal_spans(nq, 1), t, scores, values, finish)


def _attn_c(cq, kcat, cvt, mask):
    B, S, _ = cq.shape
    t = ATT_TILE
    assert S % t == 0
    wide = pl.BlockSpec((1, S, 2 * C_HEAD_PAD), lambda b, p: (b, 0, p))
    return pl.pallas_call(
        functools.partial(_attn_c_kernel, t=t, nq=S // t), grid=(B, C_HEADS // 2),
        in_specs=[wide, wide, pl.BlockSpec((1, LANES, S), lambda b, p: (b, p, 0)), _full(mask.shape)],
        out_specs=pl.BlockSpec((1, S, LANES), lambda b, p: (b, 0, p)),
        out_shape=jax.ShapeDtypeStruct((B, S, C_HEADS * C_VDIM), BF16),
        compiler_params=_params(2), name="attn_mla")(cq, kcat, cvt, mask)


def _dense_attn_kernel(*refs, mode):
    if mode == 'a':
        lam_ref, subln_ref = refs[:2]
        refs = refs[2:]
    q_ref, kc_ref, kn_ref, vc_ref, vn_ref, bc_ref, bn_ref, o_ref = refs
    q = q_ref[0]
    parts = [(kc_ref[0].astype(BF16), _with_ones(vc_ref[0].astype(BF16)), bc_ref),
             (kn_ref[0], _with_ones(vn_ref[0]), bn_ref)]
    if mode == 'a':
        streams = [(qm, slice(None), s // 2) for s, qm in enumerate(_masked_streams(q, A_HALF))]
    elif mode == 'b':
        streams = [(qm, slice(None), s) for s, qm in enumerate(_masked_streams(q, B_DIM))]
    else:
        streams = [(q[:, s * C_HEAD_PAD:(s + 1) * C_HEAD_PAD], slice(s * C_HEAD_PAD, (s + 1) * C_HEAD_PAD), s)
                   for s in range(2)]
    outs = []
    for qm, lanes, hh in streams:
        scs = [_apply_bias(_dot_nt(qm, k[:, lanes]), b_ref[hh]) for k, _, b_ref in parts]
        m = functools.reduce(jnp.maximum, [jnp.max(sc, axis=1, keepdims=True) for sc in scs])
        acc = sum(_dot(jnp.exp2(sc - m).astype(BF16), v1[hh]) for sc, (_, v1, _) in zip(scs, parts))
        outs.append(_normalise(acc))
    if mode == 'a':
        lam, lam_init = _diff_lambda(lam_ref)
        o = jnp.where(_lane_iota() < A_VDIM, outs[0] - lam * outs[1], outs[2] - lam * outs[3])
        o_ref[0] = _finish_diff(o, lam_init, subln_ref).astype(BF16)
    else:
        o_ref[0] = _first_lanes(outs)


def _dense_attn(mode, q, k_cache, k_new, v_cache, v_new, bias, n_heads, extra=()):
    B, T, _ = q.shape
    P = k_cache.shape[1]
    qw = 2 * C_HEAD_PAD if mode == 'c' else LANES
    blk = lambda rows, w: pl.BlockSpec((1, rows, w), lambda b, p: (b, 0, p))
    kernel = functools.partial(_dense_attn_kernel, mode=mode)
    return pl.pallas_call(
        kernel, grid=(B, n_heads // 2),
        in_specs=[_full(e.shape) for e in extra] + [
            blk(T, qw), blk(P, qw), blk(T, qw), blk(P, LANES), blk(T, LANES),
            pl.BlockSpec((2, T, P), lambda b, p: (p, 0, 0)), pl.BlockSpec((2, T, T), lambda b, p: (p, 0, 0))],
        out_specs=blk(T, LANES),
        out_shape=jax.ShapeDtypeStruct((B, T, n_heads * 64), BF16),
        compiler_params=_params(2), name="attn_dense_" + mode)(
            *extra, q, k_cache, k_new, v_cache, v_new, bias[:, :, :P], bias[:, :, P:])


def _memkv_kernel(mem_ref, g_ref, wk_ref, wv_ref, k32_o, k16_o, v32_o, v16_o):
    m = _rms(mem_ref[0], g_ref[...]).astype(BF16)
    mk = _dot(m, wk_ref[...])
    mv = _dot(m, wv_ref[...])
    k32_o[0] = mk
    k16_o[0] = mk.astype(BF16)
    v32_o[0] = mv
    v16_o[0] = mv.astype(BF16)


def _memkv(mem, g, wk, wv):
    B = mem.shape[0]
    n = M_HEADS * M_DIM
    blk = pl.BlockSpec((1, MEM_LEN, n), lambda b: (b, 0, 0))
    return pl.pallas_call(
        _memkv_kernel, grid=(B,),
        in_specs=[pl.BlockSpec((1, MEM_LEN, D_MODEL), lambda b: (b, 0, 0)), _full(g.shape), _full(wk.shape),
                  _full(wv.shape)],
        out_specs=[blk] * 4,
        out_shape=[jax.ShapeDtypeStruct((B, MEM_LEN, n), d) for d in (F32, BF16, F32, BF16)],
        compiler_params=_params(1), name="mem_kv")(mem, g, wk, wv)


def _tail_kernel(x_ref, xh_ref, a_ref, ah_ref, b_ref, bh_ref, c_ref, ch_ref, mk_ref, mv_ref, past_ref,
                 wout_ref, gmix_ref, gxpre_ref, wq_ref, wo_ref, gxpost_ref,
                 gfpre_ref, wg_ref, wu_ref, cw_ref, cb_ref, wd_ref, gfpost_ref, o_ref, st_ref, *, tm):
    rows = HALO + tm
    first = pl.program_id(1) == 0
    cat = lambda halo_ref, ref: jnp.concatenate([halo_ref[0], ref[0]], axis=0)
    x = cat(xh_ref, x_ref)
    na = A_HEADS * A_VDIM
    nb = na + B_HEADS * B_DIM
    y = (_dot(cat(ah_ref, a_ref), wout_ref[0:na, :]) + _dot(cat(bh_ref, b_ref), wout_ref[na:nb, :])
         + _dot(cat(ch_ref, c_ref), wout_ref[nb:D_MODEL, :]))
    x = x + _rms(y, gmix_ref[...])
    h = _rms(x, gxpre_ref[...]).astype(BF16)
    q = (_dot(h, wq_ref[...]) * (M_SCALE * LOG2E)).astype(BF16)
    pairs = []
    for p in range(M_HEADS // 2):
        sl = slice(p * LANES, (p + 1) * LANES)
        kp = mk_ref[0, :, sl]
        vp1 = _with_ones(mv_ref[0, :, sl])
        outs = [_softmax_pv(_dot_nt(qm, kp), vp1[s]) for s, qm in enumerate(_masked_streams(q[:, sl], M_DIM))]
        pairs.append(_first_lanes(outs))
    x = x + _rms(_dot(jnp.concatenate(pairs, axis=1), wo_ref[...]), gxpost_ref[...])
    h = _rms(x, gfpre_ref[...]).astype(BF16)
    row = lax.broadcasted_iota(jnp.int32, (rows, 1), 0)
    past_row = [jnp.logical_and(first, row == HALO - 2), jnp.logical_and(first, row == HALO - 1)]
    bounds = list(range(0, D_FF, FF_CHUNK)) + [D_FF]
    chunks = [slice(lo, hi) for lo, hi in zip(bounds[:-1], bounds[1:])]

    def up(sl):
        return _dot(h, wg_ref[:, sl]), _dot(h, wu_ref[:, sl])

    acc = jnp.zeros((rows, D_MODEL), F32)
    nxt = up(chunks[0])
    for c, sl in enumerate(chunks):
        g, u = nxt
        if c + 1 < len(chunks):
            nxt = up(chunks[c + 1])
        past = past_ref[0, :, sl]
        g = jnp.where(past_row[0], past[0:1], jnp.where(past_row[1], past[1:2], g))
        cw = cw_ref[:, sl]
        cc = (cb_ref[:, sl] + pltpu.roll(g, 2, axis=0) * cw[0:1] + pltpu.roll(g, 1, axis=0) * cw[1:2]
              + g * cw[2:3])
        acc = acc + _dot((cc * jax.nn.sigmoid(cc) * u).astype(BF16), wd_ref[sl, :])
        st_ref[0, :, sl] = g[rows - 2:rows]
    o_ref[0] = (x + _rms(acc, gfpost_ref[...]))[HALO:]


def _tail(x, a, b, c, mk, mv, past, lw):
    B, S, _ = x.shape
    tm = _row_tile(S, 512)
    assert tm % HALO == 0
    hb = tm // HALO
    row = lambda w: pl.BlockSpec((1, tm, w), lambda bb, i: (bb, i, 0))
    halo = lambda w: pl.BlockSpec((1, HALO, w), lambda bb, i: (bb, jnp.maximum(i * hb - 1, 0), 0))
    mem = pl.BlockSpec((1, MEM_LEN, M_HEADS * M_DIM), lambda bb, i: (bb, 0, 0))
    st = pl.BlockSpec((1, CONV_W - 1, D_FF), lambda bb, i: (bb, 0, 0))
    ws = [lw[k] for k in ('w_out', 'g_mix_post', 'g_x_pre', 'w_xq', 'w_xo', 'g_x_post',
                          'g_f_pre', 'w_gate', 'w_up', 'conv_w', 'conv_b', 'w_down', 'g_f_post')]
    acts = []
    specs = []
    for arr in (x, a, b, c):
        acts += [arr, arr]
        specs += [row(arr.shape[2]), halo(arr.shape[2])]
    return pl.pallas_call(
        functools.partial(_tail_kernel, tm=tm), grid=(B, S // tm),
        in_specs=specs + [mem, mem, st] + [_resident(w.shape) for w in ws],
        out_specs=[row(D_MODEL), st],
        out_shape=[jax.ShapeDtypeStruct(x.shape, F32), jax.ShapeDtypeStruct((B, CONV_W - 1, D_FF), F32)],
        compiler_params=_params(2), name="tail")(*acts, mk, mv, past, *ws)


def _rot_cols(w):
    half = C_ROPE // 2
    return jnp.concatenate([-w[..., half:], w[..., :half]], axis=-1)


def _layer_weights(l, w_in, w_out, norms, mla_q_norm, mla_w_uq, mla_kv_norm, mla_w_ukv, w_xq, w_mk, w_mv, w_xo,
                   w_gate, w_up, conv_w, conv_b, w_down):
    wi = w_in[l]
    na, nb = 3 * A_HEADS * 2 * A_HALF, 3 * B_HEADS * B_DIM
    wcq = wi[:, na + nb:na + nb + C_Q_LORA]
    wckv = wi[:, na + nb + C_Q_LORA:na + nb + C_Q_LORA + C_KV_LORA]
    wcpe = wi[:, na + nb + C_Q_LORA + C_KV_LORA:]
    wc = jnp.concatenate([wcq, wckv, wcpe, _rot_cols(wcpe),
                          jnp.zeros((D_MODEL, 512 - C_Q_LORA - C_KV_LORA - 2 * C_ROPE), F32)], axis=1)
    uq = mla_w_uq[l].reshape(C_Q_LORA, C_HEADS, C_NOPE + C_ROPE)
    uq_n = jnp.pad(uq[..., :C_NOPE], ((0, 0), (0, 0), (0, C_HEAD_PAD - C_NOPE))).reshape(C_Q_LORA, -1)
    uq_p = uq[..., C_NOPE:]
    wuq = jnp.concatenate([uq_n, uq_p.reshape(C_Q_LORA, -1), _rot_cols(uq_p).reshape(C_Q_LORA, -1)], axis=1)
    eye = jnp.eye(C_ROPE, dtype=F32)
    place = jnp.pad(eye, ((0, 0), (C_NOPE, C_HEAD_PAD - C_NOPE - C_ROPE)))
    eq = jnp.kron(jnp.eye(C_HEADS, dtype=F32), place)
    ek = jnp.tile(place, (1, C_HEADS))
    ukv = mla_w_ukv[l].reshape(C_KV_LORA, C_HEADS, C_NOPE + C_VDIM)
    wkn = jnp.pad(ukv[..., :C_NOPE], ((0, 0), (0, 0), (0, C_HEAD_PAD - C_NOPE))).reshape(C_KV_LORA, -1)
    wcv = ukv[..., C_NOPE:].reshape(C_KV_LORA, -1)
    g = norms[l]
    row = lambda v: v.reshape(1, -1).astype(F32)
    bf = lambda v: v.astype(BF16)
    return dict(
        wa=bf(wi[:, :na]), wb=bf(wi[:, na:na + nb]), wc=bf(wc), wuq=bf(wuq), eq=bf(eq), ek=bf(ek),
        wkn=bf(wkn), wcv=bf(wcv), g_q=row(mla_q_norm[l]), g_kv=row(mla_kv_norm[l]),
        g_mix_pre=row(g[N_MIX_PRE]), g_mix_post=row(g[N_MIX_POST]), g_x_pre=row(g[N_X_PRE]),
        g_x_post=row(g[N_X_POST]), g_f_pre=row(g[N_F_PRE]), g_f_post=row(g[N_F_POST]), g_mem=row(g[N_MEM]),
        w_out=bf(w_out[l]), w_xq=bf(w_xq[l]), w_mk=bf(w_mk[l]), w_mv=bf(w_mv[l]), w_xo=bf(w_xo[l]),
        w_gate=bf(w_gate[l]), w_up=bf(w_up[l]), w_down=bf(w_down[l]),
        conv_w=conv_w[l].astype(F32), conv_b=row(conv_b[l]))


def _rope_tables(pos):
    half = C_ROPE // 2
    inv = ROPE_THETA ** (-jnp.arange(half, dtype=F32) / half)
    ang = pos.astype(F32)[:, None] * inv[None, :]
    reps = LANES // half
    return jnp.tile(jnp.cos(ang), (1, reps)), jnp.tile(jnp.sin(ang), (1, reps))


def _t5_bucket(rel):
    half = T5_BUCKETS // 2
    exact = half // 2
    n = jnp.abs(rel)
    n2 = jnp.maximum(n * n, 1)
    log2_n2 = 31 - lax.clz(n2.astype(jnp.int32))
    large = jnp.minimum(log2_n2 + 2, half - 1)
    return jnp.where(rel > 0, half, 0) + jnp.where(n < exact, n, large)


def _chunk_causal(qpos, kpos):
    return (kpos[None, :] // CHUNK) <= (qpos[:, None] // CHUNK)


def _t5_lookup(t5_table):
    return lambda rel: t5_table[_t5_bucket(rel)].astype(F32).T


def _band_lookup(rel_table):
    return lambda rel: rel_table[:, jnp.clip(rel, -B_REL_CLIP, B_REL_CLIP) + B_REL_CLIP].astype(F32)


def _toeplitz_tile(lookup, t, d):
    m = jnp.arange(2 * t, dtype=jnp.int32)
    v = lookup(d + m - 2 * t * (m >= t))
    h = v.shape[0]
    strip = jnp.broadcast_to(v[:, None, :], (h, t, 2 * t)).reshape(h, 2 * t * t)
    return strip[:, :t * (2 * t - 1)].reshape(h, t, 2 * t - 1)[:, :, :t]


def _toeplitz_rows(lookup, q0, nq, k0, nk):
    n = nk + nq
    w = lookup(k0 - q0 - (nq - 1) + jnp.arange(n, dtype=jnp.int32))
    h = w.shape[0]
    strip = jnp.broadcast_to(w[:, None, :], (h, nq, n)).reshape(h, nq * n)
    return strip[:, :nq * (n - 1)].reshape(h, nq, n - 1)[:, :, nq - 1:nq - 1 + nk]


def _band_mask(qpos, kpos):
    qc = qpos[:, None] // CHUNK
    kc = kpos[None, :] // CHUNK
    return (kpos[None, :] >= 0) & (kc <= qc) & (kc >= qc - B_LEFT_CHUNKS)


def _masked(bias, mask):
    return jnp.where(mask, bias * LOG2E, NEG)


def _layer_prompt(x, mem, lw, lam, subln2, t5_table, rel_table):
    B, S, _ = x.shape
    t = ATT_TILE
    pos = jnp.arange(S, dtype=jnp.int32)
    cos4, sin4 = _rope_tables(pos)
    (aq, ak32, ak16, av32, avt, bq, bk32, bk16, bv32, bvt, cq, lat, kpe, kcat, cvt) = _inproj(x, lw, cos4, sin4, True)
    tpos = jnp.arange(t, dtype=jnp.int32)
    causal = _chunk_causal(tpos, tpos)
    assert t + 1 >= T5_MAX_DIST
    t5 = _t5_lookup(t5_table)
    far = t5(jnp.full((1,), -(t + 1), jnp.int32))[:, :, None]
    kq = lambda tiles: jnp.swapaxes(tiles, -1, -2)
    bias_a = jnp.stack([_masked(_toeplitz_tile(t5, t, 0) - far, causal[None]),
                        (_toeplitz_tile(t5, t, -t) - far) * LOG2E], axis=1)
    a = _attn_a(aq, ak16, avt, kq(bias_a), lam, subln2)
    n_back = -(-B_LEFT_CHUNKS * CHUNK // t)
    band = _band_lookup(rel_table)
    bias_b = jnp.stack([_masked(_toeplitz_tile(band, t, -k * t),
                                _band_mask(tpos + n_back * t, tpos + (n_back - k) * t)[None])
                        for k in range(n_back + 1)], axis=1)
    b = _attn_b(bq, bk16, bvt, kq(bias_b))
    c = _attn_c(cq, kcat, cvt, kq(_masked(jnp.zeros((t, t), F32), causal)))
    mk32, mk16, mv32, mv16 = _memkv(mem, lw['g_mem'], lw['w_mk'], lw['w_mv'])
    x, conv_state = _tail(x, a, b, c, mk16, mv16, jnp.zeros((B, CONV_W - 1, D_FF), F32), lw)
    nb = min(B_LEFT_CHUNKS * CHUNK, S)
    state = (ak32.reshape(B, S, A_HEADS, 2 * A_HALF), av32.reshape(B, S, A_HEADS, A_VDIM),
             bk32[:, S - nb:].reshape(B, nb, B_HEADS, B_DIM), bv32[:, S - nb:].reshape(B, nb, B_HEADS, B_DIM),
             lat, kpe, mk32.reshape(B, MEM_LEN, M_HEADS, M_DIM), mv32.reshape(B, MEM_LEN, M_HEADS, M_DIM),
             conv_state)
    return x, state


def _layer_sample(x, cache, lw, lam, subln2, t5_table, rel_table):
    ak_c, av_c, bk_c, bv_c, lat_c, kpe_c, mk_c, mv_c, conv_c = cache
    B, T, _ = x.shape
    P = ak_c.shape[1]
    nb = bk_c.shape[1]
    pos = P + jnp.arange(T, dtype=jnp.int32)
    cos4, sin4 = _rope_tables(pos)
    (aq, ak32, ak16, av32, av16, bq, bk32, bk16, bv32, bv16, cq, lat, kpe, kcat, cv) = _inproj(x, lw, cos4, sin4, False)
    kpos = jnp.arange(P + T, dtype=jnp.int32)
    causal = _chunk_causal(pos, kpos)
    bias_a = _masked(_toeplitz_rows(_t5_lookup(t5_table), P, T, 0, P + T), causal[None])
    a = _dense_attn('a', aq, ak_c.reshape(B, P, -1), ak16, av_c.reshape(B, P, -1), av16, bias_a, A_HEADS,
                    extra=(lam, subln2))
    bkpos = (P - nb) + jnp.arange(nb + T, dtype=jnp.int32)
    bias_b = _masked(_toeplitz_rows(_band_lookup(rel_table), P, T, P - nb, nb + T), _band_mask(pos, bkpos)[None])
    b = _dense_attn('b', bq, bk_c.reshape(B, nb, -1), bk16, bv_c.reshape(B, nb, -1), bv16, bias_b, B_HEADS)
    kcat_c, cv_c = _expand_cache(lat_c, kpe_c, lw)
    mask_c = jnp.broadcast_to(_masked(jnp.zeros((T, P + T), F32), causal)[None], (C_HEADS, T, P + T))
    c = _dense_attn('c', cq, kcat_c, kcat, cv_c, cv, mask_c, C_HEADS)
    x, conv_state = _tail(x, a, b, c, mk_c.reshape(B, MEM_LEN, -1).astype(BF16), mv_c.reshape(B, MEM_LEN, -1).astype(BF16),
                          conv_c.astype(F32), lw)
    state = (ak32.reshape(B, T, A_HEADS, 2 * A_HALF), av32.reshape(B, T, A_HEADS, A_VDIM),
             bk32.reshape(B, T, B_HEADS, B_DIM), bv32.reshape(B, T, B_HEADS, B_DIM), lat, kpe, conv_state)
    return x, state


def kernel(x_prompt, x_sample, cache_a_k, cache_a_v, cache_b_k, cache_b_v, cache_c_latent, cache_c_rope_k, cache_mem_k, cache_mem_v, state_ffn_conv, mem_prompt, w_in, w_out, norms, diff_lambda, diff_subln, t5_bias, band_rel_bias, mla_q_norm, mla_w_uq, mla_kv_norm, mla_w_ukv, w_xq, w_mk, w_mv, w_xo, w_gate, w_up, conv_w, conv_b, w_down):
    xp, xs = x_prompt, x_sample
    sp_all, ss_all = [], []
    for l in range(DEPTH):
        lw = _layer_weights(l, w_in, w_out, norms, mla_q_norm, mla_w_uq, mla_kv_norm, mla_w_ukv, w_xq, w_mk, w_mv,
                            w_xo, w_gate, w_up, conv_w, conv_b, w_down)
        lam_init = 0.8 - 0.6 * math.exp(-0.3 * l)
        lam = jnp.concatenate([diff_lambda[l].astype(F32), jnp.full((1, A_HALF), lam_init, F32)], axis=0)
        subln2 = jnp.tile(diff_subln[l].astype(F32).reshape(1, -1), (1, LANES // A_VDIM))
        xp, sp = _layer_prompt(xp, mem_prompt, lw, lam, subln2, t5_bias, band_rel_bias[l])
        cache_l = (cache_a_k[l], cache_a_v[l], cache_b_k[l], cache_b_v[l], cache_c_latent[l],
                   cache_c_rope_k[l], cache_mem_k[l], cache_mem_v[l], state_ffn_conv[l])
        xs, ss = _layer_sample(xs, cache_l, lw, lam, subln2, t5_bias, band_rel_bias[l])
        sp_all.append(sp)
        ss_all.append(ss)

    def stk(states, i):
        return jnp.stack([s[i] for s in states])

    return (xp, xs) + tuple(stk(sp_all, i) for i in range(9)) + tuple(stk(ss_all, i) for i in range(7))
```

```python
import functools
import math

import jax
import jax.numpy as jnp
import numpy as np
from jax import lax
from jax.experimental import pallas as pl
from jax.experimental.pallas import tpu as pltpu

D_MODEL = 1024
DEPTH = 2
CHUNK = 64
A_HEADS, A_HALF, A_VDIM = 8, 32, 64
B_HEADS, B_DIM, B_LEFT_CHUNKS, B_REL_CLIP = 4, 64, 8, 128
C_HEADS, C_NOPE, C_ROPE, C_VDIM, C_Q_LORA, C_KV_LORA = 4, 64, 32, 64, 256, 128
ROPE_THETA = 10000.0
T5_BUCKETS = 32
T5_MAX_DIST = 128
MEM_LEN, M_HEADS, M_DIM = 256, 4, 64
D_FF = 2816
CONV_W = 3
EPS = 1e-6
NEG = -1e30
N_MIX_PRE, N_MIX_POST, N_X_PRE, N_X_POST, N_F_PRE, N_F_POST, N_MEM = 0, 1, 2, 3, 4, 5, 6

LOG2E = math.log2(math.e)
A_SCALE = A_HALF ** -0.5 * LOG2E
B_SCALE = B_DIM ** -0.5 * LOG2E
C_SCALE = (C_NOPE + C_ROPE) ** -0.5 * LOG2E
M_SCALE = M_DIM ** -0.5

LANES = 128
C_HEAD_PAD = 128
ATT_TILE = 256
FAR_SPAN = 4
FF_CHUNK = 512
HALO = 16
VMEM_LIMIT = 56 * 1024 * 1024

BF16 = jnp.bfloat16
F32 = jnp.float32


def _params(n_grid):
    return pltpu.CompilerParams(dimension_semantics=("arbitrary",) * n_grid, vmem_limit_bytes=VMEM_LIMIT)


def _dot(a, b):
    return jnp.dot(a, b, preferred_element_type=F32)


def _dot_nt(a, b):
    return lax.dot_general(a, b, (((1,), (1,)), ((), ())), preferred_element_type=F32)


def _rms(x, g):
    ms = jnp.mean(x * x, axis=-1, keepdims=True)
    return x * lax.rsqrt(ms + EPS) * g


def _full(shape):
    n = len(shape)
    return pl.BlockSpec(shape, lambda *_: (0,) * n)


def _resident(shape):
    n = len(shape)
    return pl.BlockSpec(shape, lambda *_: (0,) * n, pipeline_mode=pl.Buffered(1))


def _row_tile(rows, cap):
    t = min(rows, cap)
    assert rows % t == 0
    return t


def _mla_expand(lat, kpe, wkn_ref, ek_ref, wcv_ref):
    lat16 = lat.astype(BF16)
    kcat = _dot(lat16, wkn_ref[...]) + _dot(kpe.astype(BF16), ek_ref[...])
    return kcat.astype(BF16), _dot(lat16, wcv_ref[...])


def _inproj_kernel(x_ref, g_ref, cos_ref, sin_ref, wa_ref, wb_ref, wc_ref, gq_ref, wuq_ref, eq_ref,
                   gkv_ref, wkn_ref, ek_ref, wcv_ref,
                   aq_o, ak32_o, ak16_o, av32_o, av16_o, bq_o, bk32_o, bk16_o, bv32_o, bv16_o,
                   cq_o, lat_o, kpe_o, kcat_o, cv_o, *, seq_minor):
    fm = (lambda v: v.T) if seq_minor else (lambda v: v)
    h = _rms(x_ref[0], g_ref[...]).astype(BF16)
    na = A_HEADS * 2 * A_HALF
    aq_o[0] = (_dot(h, wa_ref[:, 0:na]) * A_SCALE).astype(BF16)
    ak = _dot(h, wa_ref[:, na:2 * na])
    ak32_o[0] = fm(ak)
    ak16_o[0] = ak.astype(BF16)
    av = fm(_dot(h, wa_ref[:, 2 * na:3 * na]))
    av32_o[0] = av
    av16_o[0] = av.astype(BF16)
    nb = B_HEADS * B_DIM
    bq_o[0] = (_dot(h, wb_ref[:, 0:nb]) * B_SCALE).astype(BF16)
    bk = _dot(h, wb_ref[:, nb:2 * nb])
    bk32_o[0] = fm(bk)
    bk16_o[0] = bk.astype(BF16)
    bv = fm(_dot(h, wb_ref[:, 2 * nb:3 * nb]))
    bv32_o[0] = bv
    bv16_o[0] = bv.astype(BF16)
    zc = _dot(h, wc_ref[...])
    cos4 = cos_ref[...]
    sin4 = sin_ref[...]
    hq = _rms(zc[:, 0:C_Q_LORA], gq_ref[...]).astype(BF16)
    nq = C_HEADS * C_HEAD_PAD
    zq = _dot(hq, wuq_ref[...])
    qpe = (zq[:, nq:nq + LANES] * cos4 + zq[:, nq + LANES:nq + 2 * LANES] * sin4) * C_SCALE
    cq_o[0] = (zq[:, 0:nq] * C_SCALE + _dot(qpe.astype(BF16), eq_ref[...])).astype(BF16)
    lat = _rms(zc[:, C_Q_LORA:C_Q_LORA + C_KV_LORA], gkv_ref[...])
    lat_o[0] = lat
    o = C_Q_LORA + C_KV_LORA
    kpe = zc[:, o:o + C_ROPE] * cos4[:, 0:C_ROPE] + zc[:, o + C_ROPE:o + 2 * C_ROPE] * sin4[:, 0:C_ROPE]
    kpe_o[0] = kpe
    kcat, cv = _mla_expand(lat, kpe, wkn_ref, ek_ref, wcv_ref)
    kcat_o[0] = kcat
    cv_o[0] = fm(cv).astype(BF16)


def _inproj(x, lw, cos4, sin4, seq_minor):
    B, S, _ = x.shape
    tm = _row_tile(S, 512)
    grid = (B, S // tm)
    row = lambda w: pl.BlockSpec((1, tm, w), lambda b, i: (b, i, 0))
    pos = pl.BlockSpec((tm, LANES), lambda b, i: (i, 0))
    weights = [lw['g_mix_pre'], cos4, sin4, lw['wa'], lw['wb'], lw['wc'], lw['g_q'], lw['wuq'], lw['eq'],
               lw['g_kv'], lw['wkn'], lw['ek'], lw['wcv']]
    in_specs = [row(D_MODEL)] + [pos if w is cos4 or w is sin4 else _resident(w.shape) for w in weights]
    widths = [(512, BF16), (512, F32), (512, BF16), (512, F32), (512, BF16),
              (256, BF16), (256, F32), (256, BF16), (256, F32), (256, BF16),
              (C_HEADS * C_HEAD_PAD, BF16), (C_KV_LORA, F32), (C_ROPE, F32),
              (C_HEADS * C_HEAD_PAD, BF16), (C_HEADS * C_VDIM, BF16)]
    out_specs = [row(w) for w, _ in widths]
    out_shape = [jax.ShapeDtypeStruct((B, S, w), d) for w, d in widths]
    if seq_minor:
        for n in (1, 3, 4, 6, 8, 9, 14):
            w, dt = widths[n]
            out_specs[n] = pl.BlockSpec((1, w, tm), lambda b, i: (b, 0, i))
            out_shape[n] = jax.ShapeDtypeStruct((B, w, S), dt)
    return pl.pallas_call(
        functools.partial(_inproj_kernel, seq_minor=seq_minor), grid=grid, in_specs=in_specs,
        out_specs=out_specs, out_shape=out_shape,
        compiler_params=_params(2), name="inproj")(x, *weights)


def _expand_kernel(lat_ref, kpe_ref, wkn_ref, ek_ref, wcv_ref, kcat_o, cv_o):
    kcat, cv = _mla_expand(lat_ref[0], kpe_ref[0], wkn_ref, ek_ref, wcv_ref)
    kcat_o[0] = kcat
    cv_o[0] = cv.astype(BF16)


def _expand_cache(lat, kpe, lw):
    B, P, _ = lat.shape
    tm = _row_tile(P, 512)
    row = lambda w: pl.BlockSpec((1, tm, w), lambda b, i: (b, i, 0))
    ws = [lw['wkn'], lw['ek'], lw['wcv']]
    return pl.pallas_call(
        _expand_kernel, grid=(B, P // tm),
        in_specs=[row(C_KV_LORA), row(C_ROPE)] + [_full(w.shape) for w in ws],
        out_specs=[row(C_HEADS * C_HEAD_PAD), row(C_HEADS * C_VDIM)],
        out_shape=[jax.ShapeDtypeStruct((B, P, C_HEADS * C_HEAD_PAD), BF16),
                   jax.ShapeDtypeStruct((B, P, C_HEADS * C_VDIM), BF16)],
        compiler_params=_params(2), name="mla_expand")(lat, kpe, *ws)


def _lane_iota():
    return lax.broadcasted_iota(jnp.int32, (1, LANES), 1)


def _masked_streams(q, width):
    lane = _lane_iota()
    return [jnp.where(lane // width == s, q, jnp.zeros_like(q)) for s in range(LANES // width)]


def _with_ones(vt):
    first = _lane_iota() < 64
    one = jnp.ones_like(vt)
    return [jnp.where(first, vt, one), jnp.where(first, one, vt)]


def _normalise(acc):
    return acc / pltpu.roll(acc, 64, axis=1)


def _with_ones_t(vt):
    first = lax.broadcasted_iota(jnp.int32, (LANES, 1), 0) < 64
    one = jnp.ones_like(vt)
    return [jnp.where(first, vt, one), jnp.where(first, one, vt)]


def _normalise_t(acc, head):
    return acc[0:64] / acc[64:128] if head == 0 else acc[64:128] / acc[0:64]


def _online_update_t(scs, vts, ms, accs):
    ps, alphas, m_out = [], [], []
    for sc, m_prev in zip(scs, ms):
        m_next = jnp.maximum(m_prev, jnp.max(sc, axis=0, keepdims=True)).astype(BF16)
        ps.append(jnp.exp2(sc.astype(BF16) - m_next))
        m_next = m_next.astype(F32)
        alphas.append(jnp.exp2(m_prev - m_next))
        m_out.append(m_next)
    return m_out, [alpha * acc + _dot(vt1, p) for p, alpha, acc, vt1 in zip(ps, alphas, accs, vts)]


def _run_tiles(tiles, rows, scores, values, finish):
    scs = scores(*tiles[0])
    ms = accs = None
    for n, (i, j, w) in enumerate(tiles):
        if n == 0 or tiles[n - 1][0] != i:
            ms = [jnp.full((1, rows), NEG, F32)] * len(scs)
            accs = [jnp.zeros((LANES, rows), F32)] * len(scs)
        cur = scs
        last = n + 1 == len(tiles)
        if not last:
            scs = scores(*tiles[n + 1])
        ms, accs = _online_update_t(cur, values(j, w), ms, accs)
        if last or tiles[n + 1][0] != i:
            finish(i, accs)


def _causal_spans(nq, n_biased):
    tiles = []
    for i in range(nq):
        n_far = max(i + 1 - n_biased, 0)
        tiles += [(i, j, min(FAR_SPAN, n_far - j)) for j in range(0, n_far, FAR_SPAN)]
        tiles += [(i, j, 1) for j in range(n_far, i + 1)]
    return tiles


def _apply_bias(sc, bt):
    return jnp.where(bt > 0.5 * NEG, sc + bt, NEG)


def _softmax_pv(sc, vt1):
    m = jnp.max(sc, axis=1, keepdims=True)
    p = jnp.exp2(sc - m).astype(BF16)
    return _normalise(_dot(p, vt1))


def _diff_lambda(lam_ref):
    lf = lam_ref[...]
    lam_init = lf[4:5, 0:1]
    lam = (jnp.exp(jnp.sum(lf[0:1] * lf[1:2], axis=1, keepdims=True))
           - jnp.exp(jnp.sum(lf[2:3] * lf[3:4], axis=1, keepdims=True)) + lam_init)
    return lam, lam_init


def _finish_diff(o, lam_init, subln_ref):
    first = _lane_iota() < A_VDIM
    sq = o * o
    ss0 = jnp.sum(jnp.where(first, sq, 0.0), axis=1, keepdims=True)
    ss1 = jnp.sum(jnp.where(first, 0.0, sq), axis=1, keepdims=True)
    ms = jnp.where(first, ss0, ss1) * (1.0 / A_VDIM)
    return (o * lax.rsqrt(ms + EPS) * subln_ref[...]) * (1.0 - lam_init)


def _tile(ref, n, t, w=1):
    return ref[0, n * t:(n + w) * t, :]


def _tile_t(ref, n, t, w=1):
    return ref[0, :, n * t:(n + w) * t]


def _pair_rows(o_h0, o_h1):
    return jnp.concatenate([o_h0, o_h1], axis=0).T


def _attn_a_kernel(lam_ref, subln_ref, q_ref, k_ref, vt_ref, bias_ref, o_ref, *, t, nq):
    lam, lam_init = _diff_lambda(lam_ref)
    qs = {}

    def scores(i, j, w):
        if i not in qs:
            qs.clear()
            qs[i] = _masked_streams(_tile(q_ref, i, t), A_HALF)
        kt = _tile(k_ref, j, t, w)
        scs = [_dot_nt(kt, q) for q in qs[i]]
        if j == i - 1:
            scs = [sc + bias_ref[s // 2, 1] for s, sc in enumerate(scs)]
        elif j == i:
            scs = [_apply_bias(sc, bias_ref[s // 2, 0]) for s, sc in enumerate(scs)]
        return scs

    def values(j, w):
        vt1 = _with_ones_t(_tile_t(vt_ref, j, t, w))
        return [vt1[s // 2] for s in range(4)]

    def finish(i, accs):
        n = [_normalise_t(acc, s // 2) for s, acc in enumerate(accs)]
        o = _pair_rows(n[0] - lam * n[1], n[2] - lam * n[3])
        o_ref[0, i * t:(i + 1) * t, :] = _finish_diff(o, lam_init, subln_ref).astype(BF16)

    _run_tiles(_causal_spans(nq, 2), t, scores, values, finish)


def _attn_a(aq, ak, avt, bias, lam5, subln2):
    B, S, _ = aq.shape
    t = ATT_TILE
    assert S % t == 0
    blk = pl.BlockSpec((1, S, LANES), lambda b, p: (b, 0, p))
    return pl.pallas_call(
        functools.partial(_attn_a_kernel, t=t, nq=S // t), grid=(B, A_HEADS // 2),
        in_specs=[_full(lam5.shape), _full(subln2.shape), blk, blk,
                  pl.BlockSpec((1, LANES, S), lambda b, p: (b, p, 0)),
                  pl.BlockSpec((2, 2, t, t), lambda b, p: (p, 0, 0, 0))],
        out_specs=blk, out_shape=jax.ShapeDtypeStruct((B, S, A_HEADS * A_VDIM), BF16),
        compiler_params=_params(2), name="attn_diff")(lam5, subln2, aq, ak, avt, bias)


def _first_lanes(outs):
    return jnp.where(_lane_iota() < 64, outs[0], outs[1]).astype(BF16)


def _attn_b_kernel(q_ref, k_ref, vt_ref, bias_ref, o_ref, *, t, nq, n_back):
    qs = {}

    def scores(i, j, w):
        if i not in qs:
            qs.clear()
            qs[i] = _masked_streams(_tile(q_ref, i, t), B_DIM)
        kt = _tile(k_ref, j, t)
        return [_apply_bias(_dot_nt(kt, q), bias_ref[s, i - j]) for s, q in enumerate(qs[i])]

    def values(j, w):
        return _with_ones_t(_tile_t(vt_ref, j, t))

    def finish(i, accs):
        o = _pair_rows(_normalise_t(accs[0], 0), _normalise_t(accs[1], 1))
        o_ref[0, i * t:(i + 1) * t, :] = o.astype(BF16)

    _run_tiles([(i, j, 1) for i in range(nq) for j in range(max(i - n_back, 0), i + 1)], t, scores, values, finish)


def _attn_b(bq, bk, bvt, bias):
    B, S, _ = bq.shape
    t = ATT_TILE
    n_back = bias.shape[1] - 1
    assert S % t == 0
    blk = pl.BlockSpec((1, S, LANES), lambda b, p: (b, 0, p))
    return pl.pallas_call(
        functools.partial(_attn_b_kernel, t=t, nq=S // t, n_back=n_back), grid=(B, B_HEADS // 2),
        in_specs=[blk, blk, pl.BlockSpec((1, LANES, S), lambda b, p: (b, p, 0)),
                  pl.BlockSpec((2, n_back + 1, t, t), lambda b, p: (p, 0, 0, 0))],
        out_specs=blk, out_shape=jax.ShapeDtypeStruct((B, S, B_HEADS * B_DIM), BF16),
        compiler_params=_params(2), name="attn_band")(bq, bk, bvt, bias)


def _attn_c_kernel(q_ref, k_ref, vt_ref, mask_ref, o_ref, *, t, nq):
    def scores(i, j, w):
        q = _tile(q_ref, i, t)
        kt = _tile(k_ref, j, t, w)
        scs = [_dot_nt(kt[:, s * C_HEAD_PAD:(s + 1) * C_HEAD_PAD], q[:, s * C_HEAD_PAD:(s + 1) * C_HEAD_PAD])
               for s in range(2)]
        return [_apply_bias(sc, mask_ref[...]) for sc in scs] if j == i else scs

    def values(j, w):
        return _with_ones_t(_tile_t(vt_ref, j, t, w))

    def finish(i, accs):
        o = _pair_rows(_normalise_t(accs[0], 0), _normalise_t(accs[1], 1))
        o_ref[0, i * t:(i + 1) * t, :] = o.astype(BF16)

    _run_tiles(_causal_spans(nq, 1), t, scores, values, finish)


def _attn_c(cq, kcat, cvt, mask):
    B, S, _ = cq.shape
    t = ATT_TILE
    assert S % t == 0
    wide = pl.BlockSpec((1, S, 2 * C_HEAD_PAD), lambda b, p: (b, 0, p))
    return pl.pallas_call(
        functools.partial(_attn_c_kernel, t=t, nq=S // t), grid=(B, C_HEADS // 2),
        in_specs=[wide, wide, pl.BlockSpec((1, LANES, S), lambda b, p: (b, p, 0)), _full(mask.shape)],
        out_specs=pl.BlockSpec((1, S, LANES), lambda b, p: (b, 0, p)),
        out_shape=jax.ShapeDtypeStruct((B, S, C_HEADS * C_VDIM), BF16),
        compiler_params=_params(2), name="attn_mla")(cq, kcat, cvt, mask)


def _dense_attn_kernel(*refs, mode):
    if mode == 'a':
        lam_ref, subln_ref = refs[:2]
        refs = refs[2:]
    q_ref, kc_ref, kn_ref, vc_ref, vn_ref, bc_ref, bn_ref, o_ref = refs
    q = q_ref[0]
    kn = kn_ref[0]
    vn1 = _with_ones(vn_ref[0])
    if mode == 'c':
        kc = kc_ref[0]
        vc1 = _with_ones(vc_ref[0])
        cache_scores = lambda qm, lanes: _dot_nt(qm, kc[:, lanes])
        cache_pv = lambda p, hh: _dot(p, vc1[hh])
    else:
        kc = kc_ref[0, 0].astype(BF16)
        vc1 = _with_ones_t(vc_ref[0, 0].astype(BF16))
        cache_scores = lambda qm, lanes: _dot(qm, kc)
        cache_pv = lambda p, hh: _dot_nt(p, vc1[hh])
    if mode == 'a':
        streams = [(qm, slice(None), s // 2) for s, qm in enumerate(_masked_streams(q, A_HALF))]
    elif mode == 'b':
        streams = [(qm, slice(None), s) for s, qm in enumerate(_masked_streams(q, B_DIM))]
    else:
        streams = [(q[:, s * C_HEAD_PAD:(s + 1) * C_HEAD_PAD], slice(s * C_HEAD_PAD, (s + 1) * C_HEAD_PAD), s)
                   for s in range(2)]
    outs = []
    for qm, lanes, hh in streams:
        sc = _apply_bias(cache_scores(qm, lanes), bc_ref[hh])
        sn = _apply_bias(_dot_nt(qm, kn[:, lanes]), bn_ref[hh])
        m = jnp.maximum(jnp.max(sc, axis=1, keepdims=True), jnp.max(sn, axis=1, keepdims=True))
        acc = cache_pv(jnp.exp2(sc - m).astype(BF16), hh) + _dot(jnp.exp2(sn - m).astype(BF16), vn1[hh])
        outs.append(_normalise(acc))
    if mode == 'a':
        lam, lam_init = _diff_lambda(lam_ref)
        o = jnp.where(_lane_iota() < A_VDIM, outs[0] - lam * outs[1], outs[2] - lam * outs[3])
        o_ref[0] = _finish_diff(o, lam_init, subln_ref).astype(BF16)
    else:
        o_ref[0] = _first_lanes(outs)


def _dense_attn(mode, q, k_cache, k_new, v_cache, v_new, bias, n_heads, extra=(), layer=0):
    B, T, _ = q.shape
    qw = 2 * C_HEAD_PAD if mode == 'c' else LANES
    blk = lambda rows, w: pl.BlockSpec((1, rows, w), lambda b, p: (b, 0, p))
    if mode == 'c':
        P = k_cache.shape[1]
        cache_specs = [blk(P, qw), blk(P, LANES)]
    else:
        P = k_cache.shape[3]
        cache_specs = [pl.BlockSpec((1, 1, LANES, P), lambda b, p: (layer, b, p, 0))] * 2
    kernel = functools.partial(_dense_attn_kernel, mode=mode)
    return pl.pallas_call(
        kernel, grid=(B, n_heads // 2),
        in_specs=[_full(e.shape) for e in extra] + [
            blk(T, qw), cache_specs[0], blk(T, qw), cache_specs[1], blk(T, LANES),
            pl.BlockSpec((2, T, P), lambda b, p: (p, 0, 0)), pl.BlockSpec((2, T, T), lambda b, p: (p, 0, 0))],
        out_specs=blk(T, LANES),
        out_shape=jax.ShapeDtypeStruct((B, T, n_heads * 64), BF16),
        compiler_params=_params(2), name="attn_dense_" + mode)(
            *extra, q, k_cache, k_new, v_cache, v_new, bias[:, :, :P], bias[:, :, P:])


def _memkv_kernel(mem_ref, g_ref, wk_ref, wv_ref, k32_o, k16_o, v32_o, v16_o):
    m = _rms(mem_ref[0], g_ref[...]).astype(BF16)
    mk = _dot(m, wk_ref[...])
    mv = _dot(m, wv_ref[...])
    k32_o[0] = mk
    k16_o[0] = mk.astype(BF16)
    v32_o[0] = mv
    v16_o[0] = mv.astype(BF16)


def _memkv(mem, g, wk, wv):
    B = mem.shape[0]
    n = M_HEADS * M_DIM
    blk = pl.BlockSpec((1, MEM_LEN, n), lambda b: (b, 0, 0))
    return pl.pallas_call(
        _memkv_kernel, grid=(B,),
        in_specs=[pl.BlockSpec((1, MEM_LEN, D_MODEL), lambda b: (b, 0, 0)), _full(g.shape), _full(wk.shape),
                  _full(wv.shape)],
        out_specs=[blk] * 4,
        out_shape=[jax.ShapeDtypeStruct((B, MEM_LEN, n), d) for d in (F32, BF16, F32, BF16)],
        compiler_params=_params(1), name="mem_kv")(mem, g, wk, wv)


def _tail_kernel(x_ref, xh_ref, a_ref, ah_ref, b_ref, bh_ref, c_ref, ch_ref, mk_ref, mv_ref, past_ref,
                 wout_ref, gmix_ref, gxpre_ref, wq_ref, wo_ref, gxpost_ref,
                 gfpre_ref, wg_ref, wu_ref, cw_ref, cb_ref, wd_ref, gfpost_ref, o_ref, st_ref, *, tm):
    rows = HALO + tm
    first = pl.program_id(1) == 0
    cat = lambda halo_ref, ref: jnp.concatenate([halo_ref[0], ref[0]], axis=0)
    x = cat(xh_ref, x_ref)
    na = A_HEADS * A_VDIM
    nb = na + B_HEADS * B_DIM
    y = (_dot(cat(ah_ref, a_ref), wout_ref[0:na, :]) + _dot(cat(bh_ref, b_ref), wout_ref[na:nb, :])
         + _dot(cat(ch_ref, c_ref), wout_ref[nb:D_MODEL, :]))
    x = x + _rms(y, gmix_ref[...])
    h = _rms(x, gxpre_ref[...]).astype(BF16)
    q = (_dot(h, wq_ref[...]) * (M_SCALE * LOG2E)).astype(BF16)
    pairs = []
    for p in range(M_HEADS // 2):
        sl = slice(p * LANES, (p + 1) * LANES)
        kp = mk_ref[0, :, sl]
        vp1 = _with_ones(mv_ref[0, :, sl])
        outs = [_softmax_pv(_dot_nt(qm, kp), vp1[s]) for s, qm in enumerate(_masked_streams(q[:, sl], M_DIM))]
        pairs.append(_first_lanes(outs))
    x = x + _rms(_dot(jnp.concatenate(pairs, axis=1), wo_ref[...]), gxpost_ref[...])
    h = _rms(x, gfpre_ref[...]).astype(BF16)
    row = lax.broadcasted_iota(jnp.int32, (rows, 1), 0)
    past_row = [jnp.logical_and(first, row == HALO - 2), jnp.logical_and(first, row == HALO - 1)]
    bounds = list(range(0, D_FF, FF_CHUNK)) + [D_FF]
    chunks = [slice(lo, hi) for lo, hi in zip(bounds[:-1], bounds[1:])]

    def up(sl):
        return _dot(h, wg_ref[:, sl]), _dot(h, wu_ref[:, sl])

    acc = jnp.zeros((rows, D_MODEL), F32)
    nxt = up(chunks[0])
    for c, sl in enumerate(chunks):
        g, u = nxt
        if c + 1 < len(chunks):
            nxt = up(chunks[c + 1])
        past = past_ref[0, :, sl]
        g = jnp.where(past_row[0], past[0:1], jnp.where(past_row[1], past[1:2], g))
        cw = cw_ref[:, sl]
        cc = (cb_ref[:, sl] + pltpu.roll(g, 2, axis=0) * cw[0:1] + pltpu.roll(g, 1, axis=0) * cw[1:2]
              + g * cw[2:3])
        acc = acc + _dot((cc * jax.nn.sigmoid(cc) * u).astype(BF16), wd_ref[sl, :])
        st_ref[0, :, sl] = g[rows - 2:rows]
    o_ref[0] = (x + _rms(acc, gfpost_ref[...]))[HALO:]


def _tail(x, a, b, c, mk, mv, past, lw):
    B, S, _ = x.shape
    tm = _row_tile(S, 512)
    assert tm % HALO == 0
    hb = tm // HALO
    row = lambda w: pl.BlockSpec((1, tm, w), lambda bb, i: (bb, i, 0))
    halo = lambda w: pl.BlockSpec((1, HALO, w), lambda bb, i: (bb, jnp.maximum(i * hb - 1, 0), 0))
    mem = pl.BlockSpec((1, MEM_LEN, M_HEADS * M_DIM), lambda bb, i: (bb, 0, 0))
    st = pl.BlockSpec((1, CONV_W - 1, D_FF), lambda bb, i: (bb, 0, 0))
    ws = [lw[k] for k in ('w_out', 'g_mix_post', 'g_x_pre', 'w_xq', 'w_xo', 'g_x_post',
                          'g_f_pre', 'w_gate', 'w_up', 'conv_w', 'conv_b', 'w_down', 'g_f_post')]
    acts = []
    specs = []
    for arr in (x, a, b, c):
        acts += [arr, arr]
        specs += [row(arr.shape[2]), halo(arr.shape[2])]
    return pl.pallas_call(
        functools.partial(_tail_kernel, tm=tm), grid=(B, S // tm),
        in_specs=specs + [mem, mem, st] + [_resident(w.shape) for w in ws],
        out_specs=[row(D_MODEL), st],
        out_shape=[jax.ShapeDtypeStruct(x.shape, F32), jax.ShapeDtypeStruct((B, CONV_W - 1, D_FF), F32)],
        compiler_params=_params(2), name="tail")(*acts, mk, mv, past, *ws)


def _rot_cols(w):
    half = C_ROPE // 2
    return jnp.concatenate([-w[..., half:], w[..., :half]], axis=-1)


def _layer_weights(l, w_in, w_out, norms, mla_q_norm, mla_w_uq, mla_kv_norm, mla_w_ukv, w_xq, w_mk, w_mv, w_xo,
                   w_gate, w_up, conv_w, conv_b, w_down):
    wi = w_in[l]
    na, nb = 3 * A_HEADS * 2 * A_HALF, 3 * B_HEADS * B_DIM
    wcq = wi[:, na + nb:na + nb + C_Q_LORA]
    wckv = wi[:, na + nb + C_Q_LORA:na + nb + C_Q_LORA + C_KV_LORA]
    wcpe = wi[:, na + nb + C_Q_LORA + C_KV_LORA:]
    wc = jnp.concatenate([wcq, wckv, wcpe, _rot_cols(wcpe),
                          jnp.zeros((D_MODEL, 512 - C_Q_LORA - C_KV_LORA - 2 * C_ROPE), F32)], axis=1)
    uq = mla_w_uq[l].reshape(C_Q_LORA, C_HEADS, C_NOPE + C_ROPE)
    uq_n = jnp.pad(uq[..., :C_NOPE], ((0, 0), (0, 0), (0, C_HEAD_PAD - C_NOPE))).reshape(C_Q_LORA, -1)
    uq_p = uq[..., C_NOPE:]
    wuq = jnp.concatenate([uq_n, uq_p.reshape(C_Q_LORA, -1), _rot_cols(uq_p).reshape(C_Q_LORA, -1)], axis=1)
    eye = jnp.eye(C_ROPE, dtype=F32)
    place = jnp.pad(eye, ((0, 0), (C_NOPE, C_HEAD_PAD - C_NOPE - C_ROPE)))
    eq = jnp.kron(jnp.eye(C_HEADS, dtype=F32), place)
    ek = jnp.tile(place, (1, C_HEADS))
    ukv = mla_w_ukv[l].reshape(C_KV_LORA, C_HEADS, C_NOPE + C_VDIM)
    wkn = jnp.pad(ukv[..., :C_NOPE], ((0, 0), (0, 0), (0, C_HEAD_PAD - C_NOPE))).reshape(C_KV_LORA, -1)
    wcv = ukv[..., C_NOPE:].reshape(C_KV_LORA, -1)
    g = norms[l]
    row = lambda v: v.reshape(1, -1).astype(F32)
    bf = lambda v: v.astype(BF16)
    return dict(
        wa=bf(wi[:, :na]), wb=bf(wi[:, na:na + nb]), wc=bf(wc), wuq=bf(wuq), eq=bf(eq), ek=bf(ek),
        wkn=bf(wkn), wcv=bf(wcv), g_q=row(mla_q_norm[l]), g_kv=row(mla_kv_norm[l]),
        g_mix_pre=row(g[N_MIX_PRE]), g_mix_post=row(g[N_MIX_POST]), g_x_pre=row(g[N_X_PRE]),
        g_x_post=row(g[N_X_POST]), g_f_pre=row(g[N_F_PRE]), g_f_post=row(g[N_F_POST]), g_mem=row(g[N_MEM]),
        w_out=bf(w_out[l]), w_xq=bf(w_xq[l]), w_mk=bf(w_mk[l]), w_mv=bf(w_mv[l]), w_xo=bf(w_xo[l]),
        w_gate=bf(w_gate[l]), w_up=bf(w_up[l]), w_down=bf(w_down[l]),
        conv_w=conv_w[l].astype(F32), conv_b=row(conv_b[l]))


def _rope_tables(pos):
    half = C_ROPE // 2
    inv = ROPE_THETA ** (-jnp.arange(half, dtype=F32) / half)
    ang = pos.astype(F32)[:, None] * inv[None, :]
    reps = LANES // half
    return jnp.tile(jnp.cos(ang), (1, reps)), jnp.tile(jnp.sin(ang), (1, reps))


def _t5_bucket(rel):
    half = T5_BUCKETS // 2
    exact = half // 2
    n = jnp.abs(rel)
    n2 = jnp.maximum(n * n, 1)
    log2_n2 = 31 - lax.clz(n2.astype(jnp.int32))
    large = jnp.minimum(log2_n2 + 2, half - 1)
    return jnp.where(rel > 0, half, 0) + jnp.where(n < exact, n, large)


def _chunk_causal(qpos, kpos):
    return (kpos[None, :] // CHUNK) <= (qpos[:, None] // CHUNK)


def _t5_lookup(t5_table):
    return lambda rel: t5_table[_t5_bucket(rel)].astype(F32).T


def _band_lookup(rel_table):
    return lambda rel: rel_table[:, jnp.clip(rel, -B_REL_CLIP, B_REL_CLIP) + B_REL_CLIP].astype(F32)


def _toeplitz_tile(lookup, t, d):
    m = jnp.arange(2 * t, dtype=jnp.int32)
    v = lookup(d + m - 2 * t * (m >= t))
    h = v.shape[0]
    strip = jnp.broadcast_to(v[:, None, :], (h, t, 2 * t)).reshape(h, 2 * t * t)
    return strip[:, :t * (2 * t - 1)].reshape(h, t, 2 * t - 1)[:, :, :t]


def _toeplitz_rows(lookup, q0, nq, k0, nk):
    n = nk + nq
    w = lookup(k0 - q0 - (nq - 1) + jnp.arange(n, dtype=jnp.int32))
    h = w.shape[0]
    strip = jnp.broadcast_to(w[:, None, :], (h, nq, n)).reshape(h, nq * n)
    return strip[:, :nq * (n - 1)].reshape(h, nq, n - 1)[:, :, nq - 1:nq - 1 + nk]


def _band_mask(qpos, kpos):
    qc = qpos[:, None] // CHUNK
    kc = kpos[None, :] // CHUNK
    return (kpos[None, :] >= 0) & (kc <= qc) & (kc >= qc - B_LEFT_CHUNKS)


def _masked(bias, mask):
    return jnp.where(mask, bias * LOG2E, NEG)


def _layer_prompt(x, mem, lw, lam, subln2, t5_table, rel_table):
    B, S, _ = x.shape
    t = ATT_TILE
    pos = jnp.arange(S, dtype=jnp.int32)
    cos4, sin4 = _rope_tables(pos)
    (aq, ak32, ak16, av32, avt, bq, bk32, bk16, bv32, bvt, cq, lat, kpe, kcat, cvt) = _inproj(x, lw, cos4, sin4, True)
    tpos = jnp.arange(t, dtype=jnp.int32)
    causal = _chunk_causal(tpos, tpos)
    assert t + 1 >= T5_MAX_DIST
    t5 = _t5_lookup(t5_table)
    far = t5(jnp.full((1,), -(t + 1), jnp.int32))[:, :, None]
    kq = lambda tiles: jnp.swapaxes(tiles, -1, -2)
    bias_a = jnp.stack([_masked(_toeplitz_tile(t5, t, 0) - far, causal[None]),
                        (_toeplitz_tile(t5, t, -t) - far) * LOG2E], axis=1)
    a = _attn_a(aq, ak16, avt, kq(bias_a), lam, subln2)
    n_back = -(-B_LEFT_CHUNKS * CHUNK // t)
    band = _band_lookup(rel_table)
    bias_b = jnp.stack([_masked(_toeplitz_tile(band, t, -k * t),
                                _band_mask(tpos + n_back * t, tpos + (n_back - k) * t)[None])
                        for k in range(n_back + 1)], axis=1)
    b = _attn_b(bq, bk16, bvt, kq(bias_b))
    c = _attn_c(cq, kcat, cvt, kq(_masked(jnp.zeros((t, t), F32), causal)))
    mk32, mk16, mv32, mv16 = _memkv(mem, lw['g_mem'], lw['w_mk'], lw['w_mv'])
    x, conv_state = _tail(x, a, b, c, mk16, mv16, jnp.zeros((B, CONV_W - 1, D_FF), F32), lw)
    nb = min(B_LEFT_CHUNKS * CHUNK, S)
    heads = lambda v, h: jnp.transpose(v.reshape(B, h, v.shape[1] // h, v.shape[2]), (0, 3, 1, 2))
    state = (heads(ak32, A_HEADS), heads(av32, A_HEADS),
             heads(bk32[:, :, S - nb:], B_HEADS), heads(bv32[:, :, S - nb:], B_HEADS),
             lat, kpe, mk32.reshape(B, MEM_LEN, M_HEADS, M_DIM), mv32.reshape(B, MEM_LEN, M_HEADS, M_DIM),
             conv_state)
    return x, state


def _feature_major(cache):
    d, b, p, h, w = cache.shape
    return jnp.transpose(cache, (0, 1, 3, 4, 2)).reshape(d, b, h * w, p)


def _layer_sample(x, layer, caches_fm, cache, lw, lam, subln2, t5_table, rel_table):
    ak_c, av_c, bk_c, bv_c = caches_fm
    lat_c, kpe_c, mk_c, mv_c, conv_c = cache
    B, T, _ = x.shape
    P = ak_c.shape[3]
    nb = bk_c.shape[3]
    pos = P + jnp.arange(T, dtype=jnp.int32)
    cos4, sin4 = _rope_tables(pos)
    (aq, ak32, ak16, av32, av16, bq, bk32, bk16, bv32, bv16, cq, lat, kpe, kcat, cv) = _inproj(x, lw, cos4, sin4, False)
    kpos = jnp.arange(P + T, dtype=jnp.int32)
    causal = _chunk_causal(pos, kpos)
    bias_a = _masked(_toeplitz_rows(_t5_lookup(t5_table), P, T, 0, P + T), causal[None])
    a = _dense_attn('a', aq, ak_c, ak16, av_c, av16, bias_a, A_HEADS, extra=(lam, subln2), layer=layer)
    bkpos = (P - nb) + jnp.arange(nb + T, dtype=jnp.int32)
    bias_b = _masked(_toeplitz_rows(_band_lookup(rel_table), P, T, P - nb, nb + T), _band_mask(pos, bkpos)[None])
    b = _dense_attn('b', bq, bk_c, bk16, bv_c, bv16, bias_b, B_HEADS, layer=layer)
    kcat_c, cv_c = _expand_cache(lat_c, kpe_c, lw)
    mask_c = jnp.broadcast_to(_masked(jnp.zeros((T, P + T), F32), causal)[None], (C_HEADS, T, P + T))
    c = _dense_attn('c', cq, kcat_c, kcat, cv_c, cv, mask_c, C_HEADS)
    x, conv_state = _tail(x, a, b, c, mk_c.reshape(B, MEM_LEN, -1).astype(BF16), mv_c.reshape(B, MEM_LEN, -1).astype(BF16),
                          conv_c.astype(F32), lw)
    state = (ak32.reshape(B, T, A_HEADS, 2 * A_HALF), av32.reshape(B, T, A_HEADS, A_VDIM),
             bk32.reshape(B, T, B_HEADS, B_DIM), bv32.reshape(B, T, B_HEADS, B_DIM), lat, kpe, conv_state)
    return x, state


def kernel(x_prompt, x_sample, cache_a_k, cache_a_v, cache_b_k, cache_b_v, cache_c_latent, cache_c_rope_k, cache_mem_k, cache_mem_v, state_ffn_conv, mem_prompt, w_in, w_out, norms, diff_lambda, diff_subln, t5_bias, band_rel_bias, mla_q_norm, mla_w_uq, mla_kv_norm, mla_w_ukv, w_xq, w_mk, w_mv, w_xo, w_gate, w_up, conv_w, conv_b, w_down):
    xp, xs = x_prompt, x_sample
    sp_all, ss_all = [], []
    caches_fm = tuple(_feature_major(c) for c in (cache_a_k, cache_a_v, cache_b_k, cache_b_v))
    for l in range(DEPTH):
        lw = _layer_weights(l, w_in, w_out, norms, mla_q_norm, mla_w_uq, mla_kv_norm, mla_w_ukv, w_xq, w_mk, w_mv,
                            w_xo, w_gate, w_up, conv_w, conv_b, w_down)
        lam_init = 0.8 - 0.6 * math.exp(-0.3 * l)
        lam = jnp.concatenate([diff_lambda[l].astype(F32), jnp.full((1, A_HALF), lam_init, F32)], axis=0)
        subln2 = jnp.tile(diff_subln[l].astype(F32).reshape(1, -1), (1, LANES // A_VDIM))
        xp, sp = _layer_prompt(xp, mem_prompt, lw, lam, subln2, t5_bias, band_rel_bias[l])
        cache_l = (cache_c_latent[l], cache_c_rope_k[l], cache_mem_k[l], cache_mem_v[l], state_ffn_conv[l])
        xs, ss = _layer_sample(xs, l, caches_fm, cache_l, lw, lam, subln2, t5_bias, band_rel_bias[l])
        sp_all.append(sp)
        ss_all.append(ss)

    def stk(states, i):
        return jnp.stack([s[i] for s in states])

    return (xp, xs) + tuple(stk(sp_all, i) for i in range(9)) + tuple(stk(ss_all, i) for i in range(7))
```

```python
import functools
import math

import jax
import jax.numpy as jnp
import numpy as np
from jax import lax
from jax.experimental import pallas as pl
from jax.experimental.pallas import tpu as pltpu

D_MODEL = 1024
DEPTH = 2
CHUNK = 64
A_HEADS, A_HALF, A_VDIM = 8, 32, 64
B_HEADS, B_DIM, B_LEFT_CHUNKS, B_REL_CLIP = 4, 64, 8, 128
C_HEADS, C_NOPE, C_ROPE, C_VDIM, C_Q_LORA, C_KV_LORA = 4, 64, 32, 64, 256, 128
ROPE_THETA = 10000.0
T5_BUCKETS = 32
T5_MAX_DIST = 128
MEM_LEN, M_HEADS, M_DIM = 256, 4, 64
D_FF = 2816
CONV_W = 3
EPS = 1e-6
NEG = -1e30
N_MIX_PRE, N_MIX_POST, N_X_PRE, N_X_POST, N_F_PRE, N_F_POST, N_MEM = 0, 1, 2, 3, 4, 5, 6

LOG2E = math.log2(math.e)
A_SCALE = A_HALF ** -0.5 * LOG2E
B_SCALE = B_DIM ** -0.5 * LOG2E
C_SCALE = (C_NOPE + C_ROPE) ** -0.5 * LOG2E
M_SCALE = M_DIM ** -0.5

LANES = 128
C_HEAD_PAD = 128
ATT_TILE = 256
FAR_SPAN = 4
FF_CHUNK = 512
HALO = 16
VMEM_LIMIT = 56 * 1024 * 1024
TAIL_VMEM_LIMIT = 60 * 1024 * 1024
TAIL_ROWS = 1024

BF16 = jnp.bfloat16
F32 = jnp.float32


def _params(n_grid, vmem=VMEM_LIMIT):
    return pltpu.CompilerParams(dimension_semantics=("arbitrary",) * n_grid, vmem_limit_bytes=vmem)


def _dot(a, b):
    return jnp.dot(a, b, preferred_element_type=F32)


def _dot_nt(a, b):
    return lax.dot_general(a, b, (((1,), (1,)), ((), ())), preferred_element_type=F32)


def _rms(x, g):
    ms = jnp.mean(x * x, axis=-1, keepdims=True)
    return x * lax.rsqrt(ms + EPS) * g


def _full(shape):
    n = len(shape)
    return pl.BlockSpec(shape, lambda *_: (0,) * n)


def _resident(shape):
    n = len(shape)
    return pl.BlockSpec(shape, lambda *_: (0,) * n, pipeline_mode=pl.Buffered(1))


def _row_tile(rows, cap):
    t = min(rows, cap)
    assert rows % t == 0
    return t


def _mla_expand(lat, kpe, wkn_ref, ek_ref, wcv_ref):
    lat16 = lat.astype(BF16)
    kcat = _dot(lat16, wkn_ref[...]) + _dot(kpe.astype(BF16), ek_ref[...])
    return kcat.astype(BF16), _dot(lat16, wcv_ref[...])


def _inproj_kernel(x_ref, g_ref, cos_ref, sin_ref, wa_ref, wb_ref, wc_ref, gq_ref, wuq_ref, eq_ref,
                   gkv_ref, wkn_ref, ek_ref, wcv_ref,
                   aq_o, ak32_o, ak16_o, av32_o, av16_o, bq_o, bk32_o, bk16_o, bv32_o, bv16_o,
                   cq_o, lat_o, kpe_o, kcat_o, cv_o, *, seq_minor):
    fm = (lambda v: v.T) if seq_minor else (lambda v: v)
    h = _rms(x_ref[0], g_ref[...]).astype(BF16)
    na = A_HEADS * 2 * A_HALF
    aq_o[0] = (_dot(h, wa_ref[:, 0:na]) * A_SCALE).astype(BF16)
    ak = _dot(h, wa_ref[:, na:2 * na])
    ak32_o[0] = fm(ak)
    ak16_o[0] = ak.astype(BF16)
    av = fm(_dot(h, wa_ref[:, 2 * na:3 * na]))
    av32_o[0] = av
    av16_o[0] = av.astype(BF16)
    nb = B_HEADS * B_DIM
    bq_o[0] = (_dot(h, wb_ref[:, 0:nb]) * B_SCALE).astype(BF16)
    bk = _dot(h, wb_ref[:, nb:2 * nb])
    bk32_o[0] = fm(bk)
    bk16_o[0] = bk.astype(BF16)
    bv = fm(_dot(h, wb_ref[:, 2 * nb:3 * nb]))
    bv32_o[0] = bv
    bv16_o[0] = bv.astype(BF16)
    zc = _dot(h, wc_ref[...])
    cos4 = cos_ref[...]
    sin4 = sin_ref[...]
    hq = _rms(zc[:, 0:C_Q_LORA], gq_ref[...]).astype(BF16)
    nq = C_HEADS * C_HEAD_PAD
    zq = _dot(hq, wuq_ref[...])
    qpe = (zq[:, nq:nq + LANES] * cos4 + zq[:, nq + LANES:nq + 2 * LANES] * sin4) * C_SCALE
    cq_o[0] = (zq[:, 0:nq] * C_SCALE + _dot(qpe.astype(BF16), eq_ref[...])).astype(BF16)
    lat = _rms(zc[:, C_Q_LORA:C_Q_LORA + C_KV_LORA], gkv_ref[...])
    lat_o[0] = lat
    o = C_Q_LORA + C_KV_LORA
    kpe = zc[:, o:o + C_ROPE] * cos4[:, 0:C_ROPE] + zc[:, o + C_ROPE:o + 2 * C_ROPE] * sin4[:, 0:C_ROPE]
    kpe_o[0] = kpe
    kcat, cv = _mla_expand(lat, kpe, wkn_ref, ek_ref, wcv_ref)
    kcat_o[0] = kcat
    cv_o[0] = fm(cv).astype(BF16)


def _inproj(x, lw, cos4, sin4, seq_minor):
    B, S, _ = x.shape
    tm = _row_tile(S, 512)
    grid = (B, S // tm)
    row = lambda w: pl.BlockSpec((1, tm, w), lambda b, i: (b, i, 0))
    pos = pl.BlockSpec((tm, LANES), lambda b, i: (i, 0))
    weights = [lw['g_mix_pre'], cos4, sin4, lw['wa'], lw['wb'], lw['wc'], lw['g_q'], lw['wuq'], lw['eq'],
               lw['g_kv'], lw['wkn'], lw['ek'], lw['wcv']]
    in_specs = [row(D_MODEL)] + [pos if w is cos4 or w is sin4 else _resident(w.shape) for w in weights]
    widths = [(512, BF16), (512, F32), (512, BF16), (512, F32), (512, BF16),
              (256, BF16), (256, F32), (256, BF16), (256, F32), (256, BF16),
              (C_HEADS * C_HEAD_PAD, BF16), (C_KV_LORA, F32), (C_ROPE, F32),
              (C_HEADS * C_HEAD_PAD, BF16), (C_HEADS * C_VDIM, BF16)]
    out_specs = [row(w) for w, _ in widths]
    out_shape = [jax.ShapeDtypeStruct((B, S, w), d) for w, d in widths]
    if seq_minor:
        for n in (1, 3, 4, 6, 8, 9, 14):
            w, dt = widths[n]
            out_specs[n] = pl.BlockSpec((1, w, tm), lambda b, i: (b, 0, i))
            out_shape[n] = jax.ShapeDtypeStruct((B, w, S), dt)
    return pl.pallas_call(
        functools.partial(_inproj_kernel, seq_minor=seq_minor), grid=grid, in_specs=in_specs,
        out_specs=out_specs, out_shape=out_shape,
        compiler_params=_params(2), name="inproj")(x, *weights)


def _expand_kernel(lat_ref, kpe_ref, wkn_ref, ek_ref, wcv_ref, kcat_o, cv_o):
    kcat, cv = _mla_expand(lat_ref[0], kpe_ref[0], wkn_ref, ek_ref, wcv_ref)
    kcat_o[0] = kcat
    cv_o[0] = cv.astype(BF16)


def _expand_cache(lat, kpe, lw):
    B, P, _ = lat.shape
    tm = _row_tile(P, 512)
    row = lambda w: pl.BlockSpec((1, tm, w), lambda b, i: (b, i, 0))
    ws = [lw['wkn'], lw['ek'], lw['wcv']]
    return pl.pallas_call(
        _expand_kernel, grid=(B, P // tm),
        in_specs=[row(C_KV_LORA), row(C_ROPE)] + [_full(w.shape) for w in ws],
        out_specs=[row(C_HEADS * C_HEAD_PAD), row(C_HEADS * C_VDIM)],
        out_shape=[jax.ShapeDtypeStruct((B, P, C_HEADS * C_HEAD_PAD), BF16),
                   jax.ShapeDtypeStruct((B, P, C_HEADS * C_VDIM), BF16)],
        compiler_params=_params(2), name="mla_expand")(lat, kpe, *ws)


def _lane_iota():
    return lax.broadcasted_iota(jnp.int32, (1, LANES), 1)


def _masked_streams(q, width):
    lane = _lane_iota()
    return [jnp.where(lane // width == s, q, jnp.zeros_like(q)) for s in range(LANES // width)]


def _with_ones(vt):
    first = _lane_iota() < 64
    one = jnp.ones_like(vt)
    return [jnp.where(first, vt, one), jnp.where(first, one, vt)]


def _normalise(acc):
    return acc / pltpu.roll(acc, 64, axis=1)


def _with_ones_t(vt):
    first = lax.broadcasted_iota(jnp.int32, (LANES, 1), 0) < 64
    one = jnp.ones_like(vt)
    return [jnp.where(first, vt, one), jnp.where(first, one, vt)]


def _normalise_t(acc, head):
    return acc[0:64] / acc[64:128] if head == 0 else acc[64:128] / acc[0:64]


def _online_update_t(scs, vts, ms, accs):
    ps, alphas, m_out = [], [], []
    for sc, m_prev in zip(scs, ms):
        m_next = jnp.maximum(m_prev, jnp.max(sc, axis=0, keepdims=True)).astype(BF16)
        ps.append(jnp.exp2(sc.astype(BF16) - m_next))
        m_next = m_next.astype(F32)
        alphas.append(jnp.exp2(m_prev - m_next))
        m_out.append(m_next)
    return m_out, [alpha * acc + _dot(vt1, p) for p, alpha, acc, vt1 in zip(ps, alphas, accs, vts)]


def _run_tiles(tiles, rows, scores, values, finish):
    scs = scores(*tiles[0])
    ms = accs = None
    for n, (i, j, w) in enumerate(tiles):
        if n == 0 or tiles[n - 1][0] != i:
            ms = [jnp.full((1, rows), NEG, F32)] * len(scs)
            accs = [jnp.zeros((LANES, rows), F32)] * len(scs)
        cur = scs
        last = n + 1 == len(tiles)
        if not last:
            scs = scores(*tiles[n + 1])
        ms, accs = _online_update_t(cur, values(j, w), ms, accs)
        if last or tiles[n + 1][0] != i:
            finish(i, accs)


def _causal_spans(nq, n_biased):
    tiles = []
    for i in range(nq):
        n_far = max(i + 1 - n_biased, 0)
        tiles += [(i, j, min(FAR_SPAN, n_far - j)) for j in range(0, n_far, FAR_SPAN)]
        tiles += [(i, j, 1) for j in range(n_far, i + 1)]
    return tiles


def _apply_bias(sc, bt):
    return jnp.where(bt > 0.5 * NEG, sc + bt, NEG)


def _softmax_pv(sc, vt1):
    m = jnp.max(sc, axis=1, keepdims=True)
    p = jnp.exp2(sc - m).astype(BF16)
    return _normalise(_dot(p, vt1))


def _diff_lambda(lam_ref):
    lf = lam_ref[...]
    lam_init = lf[4:5, 0:1]
    lam = (jnp.exp(jnp.sum(lf[0:1] * lf[1:2], axis=1, keepdims=True))
           - jnp.exp(jnp.sum(lf[2:3] * lf[3:4], axis=1, keepdims=True)) + lam_init)
    return lam, lam_init


def _finish_diff(o, lam_init, subln_ref):
    first = _lane_iota() < A_VDIM
    sq = o * o
    ss0 = jnp.sum(jnp.where(first, sq, 0.0), axis=1, keepdims=True)
    ss1 = jnp.sum(jnp.where(first, 0.0, sq), axis=1, keepdims=True)
    ms = jnp.where(first, ss0, ss1) * (1.0 / A_VDIM)
    return (o * lax.rsqrt(ms + EPS) * subln_ref[...]) * (1.0 - lam_init)


def _tile(ref, n, t, w=1):
    return ref[0, n * t:(n + w) * t, :]


def _tile_t(ref, n, t, w=1):
    return ref[0, :, n * t:(n + w) * t]


def _pair_rows(o_h0, o_h1):
    return jnp.concatenate([o_h0, o_h1], axis=0).T


def _attn_a_kernel(lam_ref, subln_ref, q_ref, k_ref, vt_ref, bias_ref, *rest, t, nq):
    if len(rest) > 1:
        k0_ref, v0_ref, k1_ref, v1_ref, o_ref, ks_ref, vs_ref = rest
        ks_ref[0, 0] = k0_ref[0]
        ks_ref[1, 0] = k1_ref[0]
        vs_ref[0, 0] = v0_ref[0]
        vs_ref[1, 0] = v1_ref[0]
    else:
        o_ref, = rest
    lam, lam_init = _diff_lambda(lam_ref)
    qs = {}

    def scores(i, j, w):
        if i not in qs:
            qs.clear()
            qs[i] = _masked_streams(_tile(q_ref, i, t), A_HALF)
        kt = _tile(k_ref, j, t, w)
        scs = [_dot_nt(kt, q) for q in qs[i]]
        if j == i - 1:
            scs = [sc + bias_ref[s // 2, 1] for s, sc in enumerate(scs)]
        elif j == i:
            scs = [_apply_bias(sc, bias_ref[s // 2, 0]) for s, sc in enumerate(scs)]
        return scs

    def values(j, w):
        vt1 = _with_ones_t(_tile_t(vt_ref, j, t, w))
        return [vt1[s // 2] for s in range(4)]

    def finish(i, accs):
        n = [_normalise_t(acc, s // 2) for s, acc in enumerate(accs)]
        o = _pair_rows(n[0] - lam * n[1], n[2] - lam * n[3])
        o_ref[0, i * t:(i + 1) * t, :] = _finish_diff(o, lam_init, subln_ref).astype(BF16)

    _run_tiles(_causal_spans(nq, 2), t, scores, values, finish)


def _attn_a(aq, ak, avt, bias, lam5, subln2, stack=()):
    B, S, _ = aq.shape
    t = ATT_TILE
    assert S % t == 0
    blk = pl.BlockSpec((1, S, LANES), lambda b, p: (b, 0, p))
    fm = pl.BlockSpec((1, LANES, S), lambda b, p: (b, p, 0))
    out_specs = [blk]
    out_shape = [jax.ShapeDtypeStruct((B, S, A_HEADS * A_VDIM), BF16)]
    if stack:
        out_specs += [pl.BlockSpec((DEPTH, 1, LANES, S), lambda b, p: (0, b, p, 0))] * 2
        out_shape += [jax.ShapeDtypeStruct((DEPTH,) + stack[0].shape, F32)] * 2
    return pl.pallas_call(
        functools.partial(_attn_a_kernel, t=t, nq=S // t), grid=(B, A_HEADS // 2),
        in_specs=[_full(lam5.shape), _full(subln2.shape), blk, blk, fm,
                  pl.BlockSpec((2, 2, t, t), lambda b, p: (p, 0, 0, 0))] + [fm] * len(stack),
        out_specs=out_specs, out_shape=out_shape,
        compiler_params=_params(2), name="attn_diff")(lam5, subln2, aq, ak, avt, bias, *stack)


def _first_lanes(outs):
    return jnp.where(_lane_iota() < 64, outs[0], outs[1]).astype(BF16)


def _attn_b_kernel(q_ref, k_ref, vt_ref, bias_ref, o_ref, *, t, nq, n_back):
    qs = {}

    def scores(i, j, w):
        if i not in qs:
            qs.clear()
            qs[i] = _masked_streams(_tile(q_ref, i, t), B_DIM)
        kt = _tile(k_ref, j, t)
        return [_apply_bias(_dot_nt(kt, q), bias_ref[s, i - j]) for s, q in enumerate(qs[i])]

    def values(j, w):
        return _with_ones_t(_tile_t(vt_ref, j, t))

    def finish(i, accs):
        o = _pair_rows(_normalise_t(accs[0], 0), _normalise_t(accs[1], 1))
        o_ref[0, i * t:(i + 1) * t, :] = o.astype(BF16)

    _run_tiles([(i, j, 1) for i in range(nq) for j in range(max(i - n_back, 0), i + 1)], t, scores, values, finish)


def _attn_b(bq, bk, bvt, bias):
    B, S, _ = bq.shape
    t = ATT_TILE
    n_back = bias.shape[1] - 1
    assert S % t == 0
    blk = pl.BlockSpec((1, S, LANES), lambda b, p: (b, 0, p))
    return pl.pallas_call(
        functools.partial(_attn_b_kernel, t=t, nq=S // t, n_back=n_back), grid=(B, B_HEADS // 2),
        in_specs=[blk, blk, pl.BlockSpec((1, LANES, S), lambda b, p: (b, p, 0)),
                  pl.BlockSpec((2, n_back + 1, t, t), lambda b, p: (p, 0, 0, 0))],
        out_specs=blk, out_shape=jax.ShapeDtypeStruct((B, S, B_HEADS * B_DIM), BF16),
        compiler_params=_params(2), name="attn_band")(bq, bk, bvt, bias)


def _attn_c_kernel(q_ref, k_ref, vt_ref, mask_ref, o_ref, *, t, nq):
    def scores(i, j, w):
        q = _tile(q_ref, i, t)
        kt = _tile(k_ref, j, t, w)
        scs = [_dot_nt(kt[:, s * C_HEAD_PAD:(s + 1) * C_HEAD_PAD], q[:, s * C_HEAD_PAD:(s + 1) * C_HEAD_PAD])
               for s in range(2)]
        return [_apply_bias(sc, mask_ref[...]) for sc in scs] if j == i else scs

    def values(j, w):
        return _with_ones_t(_tile_t(vt_ref, j, t, w))

    def finish(i, accs):
        o = _pair_rows(_normalise_t(accs[0], 0), _normalise_t(accs[1], 1))
        o_ref[0, i * t:(i + 1) * t, :] = o.astype(BF16)

    _run_tiles(_causal_spans(nq, 1), t, scores, values, finish)


def _attn_c(cq, kcat, cvt, mask):
    B, S, _ = cq.shape
    t = ATT_TILE
    assert S % t == 0
    wide = pl.BlockSpec((1, S, 2 * C_HEAD_PAD), lambda b, p: (b, 0, p))
    return pl.pallas_call(
        functools.partial(_attn_c_kernel, t=t, nq=S // t), grid=(B, C_HEADS // 2),
        in_specs=[wide, wide, pl.BlockSpec((1, LANES, S), lambda b, p: (b, p, 0)), _full(mask.shape)],
        out_specs=pl.BlockSpec((1, S, LANES), lambda b, p: (b, 0, p)),
        out_shape=jax.ShapeDtypeStruct((B, S, C_HEADS * C_VDIM), BF16),
        compiler_params=_params(2), name="attn_mla")(cq, kcat, cvt, mask)


def _dense_attn_kernel(*refs, mode):
    if mode == 'a':
        lam_ref, subln_ref = refs[:2]
        refs = refs[2:]
    q_ref, kc_ref, kn_ref, vc_ref, vn_ref, bc_ref, bn_ref, o_ref = refs
    q = q_ref[0]
    kn = kn_ref[0]
    vn1 = _with_ones(vn_ref[0])
    if mode == 'c':
        kc = kc_ref[0]
        vc1 = _with_ones(vc_ref[0])
        cache_scores = lambda qm, lanes: _dot_nt(qm, kc[:, lanes])
        cache_pv = lambda p, hh: _dot(p, vc1[hh])
    else:
        kc = kc_ref[0, 0].astype(BF16)
        vc1 = _with_ones_t(vc_ref[0, 0].astype(BF16))
        cache_scores = lambda qm, lanes: _dot(qm, kc)
        cache_pv = lambda p, hh: _dot_nt(p, vc1[hh])
    if mode == 'a':
        streams = [(qm, slice(None), s // 2) for s, qm in enumerate(_masked_streams(q, A_HALF))]
    elif mode == 'b':
        streams = [(qm, slice(None), s) for s, qm in enumerate(_masked_streams(q, B_DIM))]
    else:
        streams = [(q[:, s * C_HEAD_PAD:(s + 1) * C_HEAD_PAD], slice(s * C_HEAD_PAD, (s + 1) * C_HEAD_PAD), s)
                   for s in range(2)]
    outs = []
    for qm, lanes, hh in streams:
        sc = _apply_bias(cache_scores(qm, lanes), bc_ref[hh])
        sn = _apply_bias(_dot_nt(qm, kn[:, lanes]), bn_ref[hh])
        m = jnp.maximum(jnp.max(sc, axis=1, keepdims=True), jnp.max(sn, axis=1, keepdims=True))
        acc = cache_pv(jnp.exp2(sc - m).astype(BF16), hh) + _dot(jnp.exp2(sn - m).astype(BF16), vn1[hh])
        outs.append(_normalise(acc))
    if mode == 'a':
        lam, lam_init = _diff_lambda(lam_ref)
        o = jnp.where(_lane_iota() < A_VDIM, outs[0] - lam * outs[1], outs[2] - lam * outs[3])
        o_ref[0] = _finish_diff(o, lam_init, subln_ref).astype(BF16)
    else:
        o_ref[0] = _first_lanes(outs)


def _dense_attn(mode, q, k_cache, k_new, v_cache, v_new, bias, n_heads, extra=(), layer=0):
    B, T, _ = q.shape
    qw = 2 * C_HEAD_PAD if mode == 'c' else LANES
    blk = lambda rows, w: pl.BlockSpec((1, rows, w), lambda b, p: (b, 0, p))
    if mode == 'c':
        P = k_cache.shape[1]
        cache_specs = [blk(P, qw), blk(P, LANES)]
    else:
        P = k_cache.shape[3]
        cache_specs = [pl.BlockSpec((1, 1, LANES, P), lambda b, p: (layer, b, p, 0))] * 2
    kernel = functools.partial(_dense_attn_kernel, mode=mode)
    return pl.pallas_call(
        kernel, grid=(B, n_heads // 2),
        in_specs=[_full(e.shape) for e in extra] + [
            blk(T, qw), cache_specs[0], blk(T, qw), cache_specs[1], blk(T, LANES),
            pl.BlockSpec((2, T, P), lambda b, p: (p, 0, 0)), pl.BlockSpec((2, T, T), lambda b, p: (p, 0, 0))],
        out_specs=blk(T, LANES),
        out_shape=jax.ShapeDtypeStruct((B, T, n_heads * 64), BF16),
        compiler_params=_params(2), name="attn_dense_" + mode)(
            *extra, q, k_cache, k_new, v_cache, v_new, bias[:, :, :P], bias[:, :, P:])


def _memkv_kernel(mem_ref, g_ref, wk_ref, wv_ref, k32_o, k16_o, v32_o, v16_o):
    m = _rms(mem_ref[0], g_ref[...]).astype(BF16)
    mk = _dot(m, wk_ref[...])
    mv = _dot(m, wv_ref[...])
    k32_o[0] = mk
    k16_o[0] = mk.astype(BF16)
    v32_o[0] = mv
    v16_o[0] = mv.astype(BF16)


def _memkv(mem, g, wk, wv):
    B = mem.shape[0]
    n = M_HEADS * M_DIM
    blk = pl.BlockSpec((1, MEM_LEN, n), lambda b: (b, 0, 0))
    return pl.pallas_call(
        _memkv_kernel, grid=(B,),
        in_specs=[pl.BlockSpec((1, MEM_LEN, D_MODEL), lambda b: (b, 0, 0)), _full(g.shape), _full(wk.shape),
                  _full(wv.shape)],
        out_specs=[blk] * 4,
        out_shape=[jax.ShapeDtypeStruct((B, MEM_LEN, n), d) for d in (F32, BF16, F32, BF16)],
        compiler_params=_params(1), name="mem_kv")(mem, g, wk, wv)


def _tail_kernel(x_ref, xh_ref, a_ref, ah_ref, b_ref, bh_ref, c_ref, ch_ref, mk_ref, mv_ref, past_ref,
                 wout_ref, gmix_ref, gxpre_ref, wq_ref, wo_ref, gxpost_ref,
                 gfpre_ref, wg_ref, wu_ref, cw_ref, cb_ref, wd_ref, gfpost_ref, o_ref, st_ref, *, tm):
    rows = HALO + tm
    first = pl.program_id(1) == 0
    cat = lambda halo_ref, ref: jnp.concatenate([halo_ref[0], ref[0]], axis=0)
    x = cat(xh_ref, x_ref)
    na = A_HEADS * A_VDIM
    nb = na + B_HEADS * B_DIM
    y = (_dot(cat(ah_ref, a_ref), wout_ref[0:na, :]) + _dot(cat(bh_ref, b_ref), wout_ref[na:nb, :])
         + _dot(cat(ch_ref, c_ref), wout_ref[nb:D_MODEL, :]))
    x = x + _rms(y, gmix_ref[...])
    h = _rms(x, gxpre_ref[...]).astype(BF16)
    q = (_dot(h, wq_ref[...]) * (M_SCALE * LOG2E)).astype(BF16)
    pairs = []
    for p in range(M_HEADS // 2):
        sl = slice(p * LANES, (p + 1) * LANES)
        kp = mk_ref[0, :, sl]
        vp1 = _with_ones(mv_ref[0, :, sl])
        outs = [_softmax_pv(_dot_nt(qm, kp), vp1[s]) for s, qm in enumerate(_masked_streams(q[:, sl], M_DIM))]
        pairs.append(_first_lanes(outs))
    x = x + _rms(_dot(jnp.concatenate(pairs, axis=1), wo_ref[...]), gxpost_ref[...])
    h = _rms(x, gfpre_ref[...]).astype(BF16)
    row = lax.broadcasted_iota(jnp.int32, (rows, 1), 0)
    past_row = [jnp.logical_and(first, row == HALO - 2), jnp.logical_and(first, row == HALO - 1)]
    bounds = list(range(0, D_FF, FF_CHUNK)) + [D_FF]
    chunks = [slice(lo, hi) for lo, hi in zip(bounds[:-1], bounds[1:])]

    def up(sl):
        return _dot(h, wg_ref[:, sl]), _dot(h, wu_ref[:, sl])

    acc = jnp.zeros((rows, D_MODEL), F32)
    nxt = up(chunks[0])
    for c, sl in enumerate(chunks):
        g, u = nxt
        if c + 1 < len(chunks):
            nxt = up(chunks[c + 1])
        past = past_ref[0, :, sl]
        g = jnp.where(past_row[0], past[0:1], jnp.where(past_row[1], past[1:2], g))
        cw = cw_ref[:, sl]
        cc = (cb_ref[:, sl] + pltpu.roll(g, 2, axis=0) * cw[0:1] + pltpu.roll(g, 1, axis=0) * cw[1:2]
              + g * cw[2:3])
        acc = acc + _dot((cc * jax.nn.sigmoid(cc) * u).astype(BF16), wd_ref[sl, :])
        st_ref[0, :, sl] = g[rows - 2:rows]
    o_ref[0] = (x + _rms(acc, gfpost_ref[...]))[HALO:]


def _tail(x, a, b, c, mk, mv, past, lw):
    B, S, _ = x.shape
    tm = _row_tile(S, TAIL_ROWS)
    assert tm % HALO == 0
    hb = tm // HALO
    row = lambda w: pl.BlockSpec((1, tm, w), lambda bb, i: (bb, i, 0))
    halo = lambda w: pl.BlockSpec((1, HALO, w), lambda bb, i: (bb, jnp.maximum(i * hb - 1, 0), 0))
    mem = pl.BlockSpec((1, MEM_LEN, M_HEADS * M_DIM), lambda bb, i: (bb, 0, 0))
    st = pl.BlockSpec((1, CONV_W - 1, D_FF), lambda bb, i: (bb, 0, 0))
    ws = [lw[k] for k in ('w_out', 'g_mix_post', 'g_x_pre', 'w_xq', 'w_xo', 'g_x_post',
                          'g_f_pre', 'w_gate', 'w_up', 'conv_w', 'conv_b', 'w_down', 'g_f_post')]
    acts = []
    specs = []
    for arr in (x, a, b, c):
        acts += [arr, arr]
        specs += [row(arr.shape[2]), halo(arr.shape[2])]
    return pl.pallas_call(
        functools.partial(_tail_kernel, tm=tm), grid=(B, S // tm),
        in_specs=specs + [mem, mem, st] + [_resident(w.shape) for w in ws],
        out_specs=[row(D_MODEL), st],
        out_shape=[jax.ShapeDtypeStruct(x.shape, F32), jax.ShapeDtypeStruct((B, CONV_W - 1, D_FF), F32)],
        compiler_params=_params(2, TAIL_VMEM_LIMIT), name="tail")(*acts, mk, mv, past, *ws)


def _rot_cols(w):
    half = C_ROPE // 2
    return jnp.concatenate([-w[..., half:], w[..., :half]], axis=-1)


def _layer_weights(l, w_in, w_out, norms, mla_q_norm, mla_w_uq, mla_kv_norm, mla_w_ukv, w_xq, w_mk, w_mv, w_xo,
                   w_gate, w_up, conv_w, conv_b, w_down):
    wi = w_in[l]
    na, nb = 3 * A_HEADS * 2 * A_HALF, 3 * B_HEADS * B_DIM
    wcq = wi[:, na + nb:na + nb + C_Q_LORA]
    wckv = wi[:, na + nb + C_Q_LORA:na + nb + C_Q_LORA + C_KV_LORA]
    wcpe = wi[:, na + nb + C_Q_LORA + C_KV_LORA:]
    wc = jnp.concatenate([wcq, wckv, wcpe, _rot_cols(wcpe),
                          jnp.zeros((D_MODEL, 512 - C_Q_LORA - C_KV_LORA - 2 * C_ROPE), F32)], axis=1)
    uq = mla_w_uq[l].reshape(C_Q_LORA, C_HEADS, C_NOPE + C_ROPE)
    uq_n = jnp.pad(uq[..., :C_NOPE], ((0, 0), (0, 0), (0, C_HEAD_PAD - C_NOPE))).reshape(C_Q_LORA, -1)
    uq_p = uq[..., C_NOPE:]
    wuq = jnp.concatenate([uq_n, uq_p.reshape(C_Q_LORA, -1), _rot_cols(uq_p).reshape(C_Q_LORA, -1)], axis=1)
    eye = jnp.eye(C_ROPE, dtype=F32)
    place = jnp.pad(eye, ((0, 0), (C_NOPE, C_HEAD_PAD - C_NOPE - C_ROPE)))
    eq = jnp.kron(jnp.eye(C_HEADS, dtype=F32), place)
    ek = jnp.tile(place, (1, C_HEADS))
    ukv = mla_w_ukv[l].reshape(C_KV_LORA, C_HEADS, C_NOPE + C_VDIM)
    wkn = jnp.pad(ukv[..., :C_NOPE], ((0, 0), (0, 0), (0, C_HEAD_PAD - C_NOPE))).reshape(C_KV_LORA, -1)
    wcv = ukv[..., C_NOPE:].reshape(C_KV_LORA, -1)
    g = norms[l]
    row = lambda v: v.reshape(1, -1).astype(F32)
    bf = lambda v: v.astype(BF16)
    return dict(
        wa=bf(wi[:, :na]), wb=bf(wi[:, na:na + nb]), wc=bf(wc), wuq=bf(wuq), eq=bf(eq), ek=bf(ek),
        wkn=bf(wkn), wcv=bf(wcv), g_q=row(mla_q_norm[l]), g_kv=row(mla_kv_norm[l]),
        g_mix_pre=row(g[N_MIX_PRE]), g_mix_post=row(g[N_MIX_POST]), g_x_pre=row(g[N_X_PRE]),
        g_x_post=row(g[N_X_POST]), g_f_pre=row(g[N_F_PRE]), g_f_post=row(g[N_F_POST]), g_mem=row(g[N_MEM]),
        w_out=bf(w_out[l]), w_xq=bf(w_xq[l]), w_mk=bf(w_mk[l]), w_mv=bf(w_mv[l]), w_xo=bf(w_xo[l]),
        w_gate=bf(w_gate[l]), w_up=bf(w_up[l]), w_down=bf(w_down[l]),
        conv_w=conv_w[l].astype(F32), conv_b=row(conv_b[l]))


def _rope_tables(pos):
    half = C_ROPE // 2
    inv = ROPE_THETA ** (-jnp.arange(half, dtype=F32) / half)
    ang = pos.astype(F32)[:, None] * inv[None, :]
    reps = LANES // half
    return jnp.tile(jnp.cos(ang), (1, reps)), jnp.tile(jnp.sin(ang), (1, reps))


def _t5_bucket(rel):
    half = T5_BUCKETS // 2
    exact = half // 2
    n = jnp.abs(rel)
    n2 = jnp.maximum(n * n, 1)
    log2_n2 = 31 - lax.clz(n2.astype(jnp.int32))
    large = jnp.minimum(log2_n2 + 2, half - 1)
    return jnp.where(rel > 0, half, 0) + jnp.where(n < exact, n, large)


def _chunk_causal(qpos, kpos):
    return (kpos[None, :] // CHUNK) <= (qpos[:, None] // CHUNK)


def _t5_lookup(t5_table):
    return lambda rel: t5_table[_t5_bucket(rel)].astype(F32).T


def _band_lookup(rel_table):
    return lambda rel: rel_table[:, jnp.clip(rel, -B_REL_CLIP, B_REL_CLIP) + B_REL_CLIP].astype(F32)


def _toeplitz_tile(lookup, t, d):
    m = jnp.arange(2 * t, dtype=jnp.int32)
    v = lookup(d + m - 2 * t * (m >= t))
    h = v.shape[0]
    strip = jnp.broadcast_to(v[:, None, :], (h, t, 2 * t)).reshape(h, 2 * t * t)
    return strip[:, :t * (2 * t - 1)].reshape(h, t, 2 * t - 1)[:, :, :t]


def _toeplitz_rows(lookup, q0, nq, k0, nk):
    n = nk + nq
    w = lookup(k0 - q0 - (nq - 1) + jnp.arange(n, dtype=jnp.int32))
    h = w.shape[0]
    strip = jnp.broadcast_to(w[:, None, :], (h, nq, n)).reshape(h, nq * n)
    return strip[:, :nq * (n - 1)].reshape(h, nq, n - 1)[:, :, nq - 1:nq - 1 + nk]


def _band_mask(qpos, kpos):
    qc = qpos[:, None] // CHUNK
    kc = kpos[None, :] // CHUNK
    return (kpos[None, :] >= 0) & (kc <= qc) & (kc >= qc - B_LEFT_CHUNKS)


def _masked(bias, mask):
    return jnp.where(mask, bias * LOG2E, NEG)


def _layer_prompt(x, mem, lw, lam, subln2, t5_table, rel_table, prev_kv=()):
    B, S, _ = x.shape
    t = ATT_TILE
    pos = jnp.arange(S, dtype=jnp.int32)
    cos4, sin4 = _rope_tables(pos)
    (aq, ak32, ak16, av32, avt, bq, bk32, bk16, bv32, bvt, cq, lat, kpe, kcat, cvt) = _inproj(x, lw, cos4, sin4, True)
    tpos = jnp.arange(t, dtype=jnp.int32)
    causal = _chunk_causal(tpos, tpos)
    assert t + 1 >= T5_MAX_DIST
    t5 = _t5_lookup(t5_table)
    far = t5(jnp.full((1,), -(t + 1), jnp.int32))[:, :, None]
    kq = lambda tiles: jnp.swapaxes(tiles, -1, -2)
    bias_a = jnp.stack([_masked(_toeplitz_tile(t5, t, 0) - far, causal[None]),
                        (_toeplitz_tile(t5, t, -t) - far) * LOG2E], axis=1)
    a = _attn_a(aq, ak16, avt, kq(bias_a), lam, subln2, stack=prev_kv + (ak32, av32) if prev_kv else ())
    stacked_kv = ()
    if prev_kv:
        a, *stacked_kv = a
    else:
        a, = a
    n_back = -(-B_LEFT_CHUNKS * CHUNK // t)
    band = _band_lookup(rel_table)
    bias_b = jnp.stack([_masked(_toeplitz_tile(band, t, -k * t),
                                _band_mask(tpos + n_back * t, tpos + (n_back - k) * t)[None])
                        for k in range(n_back + 1)], axis=1)
    b = _attn_b(bq, bk16, bvt, kq(bias_b))
    c = _attn_c(cq, kcat, cvt, kq(_masked(jnp.zeros((t, t), F32), causal)))
    mk32, mk16, mv32, mv16 = _memkv(mem, lw['g_mem'], lw['w_mk'], lw['w_mv'])
    x, conv_state = _tail(x, a, b, c, mk16, mv16, jnp.zeros((B, CONV_W - 1, D_FF), F32), lw)
    nb = min(B_LEFT_CHUNKS * CHUNK, S)
    heads = lambda v, h: jnp.transpose(v.reshape(B, h, v.shape[1] // h, v.shape[2]), (0, 3, 1, 2))
    state = (heads(ak32, A_HEADS), heads(av32, A_HEADS),
             heads(bk32[:, :, S - nb:], B_HEADS), heads(bv32[:, :, S - nb:], B_HEADS),
             lat, kpe, mk32.reshape(B, MEM_LEN, M_HEADS, M_DIM), mv32.reshape(B, MEM_LEN, M_HEADS, M_DIM),
             conv_state)
    stacked = tuple(jnp.transpose(v.reshape(DEPTH, B, A_HEADS, -1, S), (0, 1, 4, 2, 3)) for v in stacked_kv)
    return x, state, (ak32, av32), stacked


def _feature_major(cache):
    d, b, p, h, w = cache.shape
    return jnp.transpose(cache, (0, 1, 3, 4, 2)).reshape(d, b, h * w, p)


def _layer_sample(x, layer, caches_fm, cache, lw, lam, subln2, t5_table, rel_table):
    ak_c, av_c, bk_c, bv_c = caches_fm
    lat_c, kpe_c, mk_c, mv_c, conv_c = cache
    B, T, _ = x.shape
    P = ak_c.shape[3]
    nb = bk_c.shape[3]
    pos = P + jnp.arange(T, dtype=jnp.int32)
    cos4, sin4 = _rope_tables(pos)
    (aq, ak32, ak16, av32, av16, bq, bk32, bk16, bv32, bv16, cq, lat, kpe, kcat, cv) = _inproj(x, lw, cos4, sin4, False)
    kpos = jnp.arange(P + T, dtype=jnp.int32)
    causal = _chunk_causal(pos, kpos)
    bias_a = _masked(_toeplitz_rows(_t5_lookup(t5_table), P, T, 0, P + T), causal[None])
    a = _dense_attn('a', aq, ak_c, ak16, av_c, av16, bias_a, A_HEADS, extra=(lam, subln2), layer=layer)
    bkpos = (P - nb) + jnp.arange(nb + T, dtype=jnp.int32)
    bias_b = _masked(_toeplitz_rows(_band_lookup(rel_table), P, T, P - nb, nb + T), _band_mask(pos, bkpos)[None])
    b = _dense_attn('b', bq, bk_c, bk16, bv_c, bv16, bias_b, B_HEADS, layer=layer)
    kcat_c, cv_c = _expand_cache(lat_c, kpe_c, lw)
    mask_c = jnp.broadcast_to(_masked(jnp.zeros((T, P + T), F32), causal)[None], (C_HEADS, T, P + T))
    c = _dense_attn('c', cq, kcat_c, kcat, cv_c, cv, mask_c, C_HEADS)
    x, conv_state = _tail(x, a, b, c, mk_c.reshape(B, MEM_LEN, -1).astype(BF16), mv_c.reshape(B, MEM_LEN, -1).astype(BF16),
                          conv_c.astype(F32), lw)
    state = (ak32.reshape(B, T, A_HEADS, 2 * A_HALF), av32.reshape(B, T, A_HEADS, A_VDIM),
             bk32.reshape(B, T, B_HEADS, B_DIM), bv32.reshape(B, T, B_HEADS, B_DIM), lat, kpe, conv_state)
    return x, state


def kernel(x_prompt, x_sample, cache_a_k, cache_a_v, cache_b_k, cache_b_v, cache_c_latent, cache_c_rope_k, cache_mem_k, cache_mem_v, state_ffn_conv, mem_prompt, w_in, w_out, norms, diff_lambda, diff_subln, t5_bias, band_rel_bias, mla_q_norm, mla_w_uq, mla_kv_norm, mla_w_ukv, w_xq, w_mk, w_mv, w_xo, w_gate, w_up, conv_w, conv_b, w_down):
    xp, xs = x_prompt, x_sample
    sp_all, ss_all = [], []
    caches_fm = tuple(_feature_major(c) for c in (cache_a_k, cache_a_v, cache_b_k, cache_b_v))
    assert DEPTH == 2
    prev_kv = stacked_kv = ()
    for l in range(DEPTH):
        lw = _layer_weights(l, w_in, w_out, norms, mla_q_norm, mla_w_uq, mla_kv_norm, mla_w_ukv, w_xq, w_mk, w_mv,
                            w_xo, w_gate, w_up, conv_w, conv_b, w_down)
        lam_init = 0.8 - 0.6 * math.exp(-0.3 * l)
        lam = jnp.concatenate([diff_lambda[l].astype(F32), jnp.full((1, A_HALF), lam_init, F32)], axis=0)
        subln2 = jnp.tile(diff_subln[l].astype(F32).reshape(1, -1), (1, LANES // A_VDIM))
        xp, sp, prev_kv, stacked_kv = _layer_prompt(xp, mem_prompt, lw, lam, subln2, t5_bias, band_rel_bias[l],
                                                    prev_kv if l == DEPTH - 1 else ())
        cache_l = (cache_c_latent[l], cache_c_rope_k[l], cache_mem_k[l], cache_mem_v[l], state_ffn_conv[l])
        xs, ss = _layer_sample(xs, l, caches_fm, cache_l, lw, lam, subln2, t5_bias, band_rel_bias[l])
        sp_all.append(sp)
        ss_all.append(ss)

    def stk(states, i):
        return jnp.stack([s[i] for s in states])

    return ((xp, xs) + tuple(stacked_kv) + tuple(stk(sp_all, i) for i in range(2, 9))
            + tuple(stk(ss_all, i) for i in range(7)))
```
